```python
import functools
import jax, jax.numpy as jnp
from jax import lax
import numpy as np

D_MODEL = 2048
BATCH = 4
SEQ = 2048
DEPTH = 2
DEC_BATCH = 128
DEC_SEQ = 8
PAST_LEN = 2048
PAGE_SIZE = 128

N_META = 16
H_A = 8
DH = 128
W_A = H_A * DH
W_B = D_MODEL // 4
NB_LRU = 8
BW_LRU = W_B // NB_LRU
W_C = D_MODEL // 4
CONV_B = 4
CONV_C = 3
CONV_F = 3
D_FF = 5632
N_BRANCH = 3
Q_BLOCK = 128
LRU_C = 8.0
SB_BIAS_INIT = -5.0
EPS = 1e-6
SCALE = DH ** -0.5
N_IN = 3 * W_A + W_B + 3 * W_C + N_BRANCH * D_MODEL

kernel_name = 'hybrid_stickbreak_rglru_shortconv_decode_step'


def rmsnorm(x, g):
    xf = x.astype(jnp.float32)
    y = xf * lax.rsqrt(jnp.mean(xf * xf, axis=-1, keepdims=True) + EPS)
    return (y * g.astype(jnp.float32)).astype(x.dtype)


def causal_conv(x, buf, w, b=None):
    k_w = w.shape[0]
    t = x.shape[1]
    xp = jnp.concatenate([buf.astype(x.dtype), x], axis=1)
    y = xp[:, 0:t] * w[0]
    for i in range(1, k_w):
        y = y + xp[:, i:i + t] * w[i]
    if b is not None:
        y = y + b
    return y, xp[:, t:]


def stick_break(q, k, v, bias, qpos, kpos):
    z = (jnp.einsum('bqhd,bkhd->bhqk', q, k).astype(jnp.float32) * SCALE
         + bias.astype(jnp.float32)[None, :, None, None])
    mask = kpos[None, :] < qpos[:, None]
    log_keep = jnp.where(mask, jax.nn.log_sigmoid(-z), 0.0)
    between = lax.cumsum(log_keep, axis=3, reverse=True) - log_keep
    w = jnp.where(mask, jnp.exp(jax.nn.log_sigmoid(z) + between), 0.0)
    return jnp.einsum('bhqk,bkhd->bqhd', w.astype(v.dtype), v)


def prompt_attend(q, k, v, bias):
    bn, total = q.shape[0], q.shape[1]
    seq = total - N_META
    pos = jnp.arange(total, dtype=jnp.int32)
    o_meta = stick_break(q[:, :N_META], k[:, :N_META], v[:, :N_META], bias, pos[:N_META], pos[:N_META])
    n_blk = seq // Q_BLOCK
    q_blk = q[:, N_META:].reshape(bn, n_blk, Q_BLOCK, H_A, DH).transpose(1, 0, 2, 3, 4)
    pos_blk = pos[N_META:].reshape(n_blk, Q_BLOCK)
    o_blk = lax.map(lambda qp: stick_break(qp[0], k, v, bias, qp[1], pos), (q_blk, pos_blk))
    o_real = o_blk.transpose(1, 0, 2, 3, 4).reshape(bn, seq, H_A, DH)
    return jnp.concatenate([o_meta, o_real], axis=1)


def sample_attend(q, k, v, bias, past_k, past_v):
    t = q.shape[1]
    p = past_k.shape[1]
    kk = jnp.concatenate([past_k.astype(k.dtype), k], axis=1)
    vv = jnp.concatenate([past_v.astype(v.dtype), v], axis=1)
    qpos = p + jnp.arange(t, dtype=jnp.int32)
    kpos = jnp.arange(p + t, dtype=jnp.int32)
    return stick_break(q, kk, vv, bias, qpos, kpos)


def _lru_combine(c1, c2):
    a1, b1 = c1
    a2, b2 = c2
    return a1 * a2, a2 * b1 + b2


def rglru(x, h0, w_r, b_r, w_i, b_i, lam, reset_first):
    bn, t, _ = x.shape
    xf = x.astype(jnp.float32)
    xb = xf.reshape(bn, t, NB_LRU, BW_LRU)
    r = jax.nn.sigmoid(jnp.einsum('btnc,ncd->btnd', xb, w_r.astype(jnp.float32)).reshape(bn, t, W_B) + b_r.astype(jnp.float32))
    i = jax.nn.sigmoid(jnp.einsum('btnc,ncd->btnd', xb, w_i.astype(jnp.float32)).reshape(bn, t, W_B) + b_i.astype(jnp.float32))
    log_a = -LRU_C * r * jax.nn.softplus(-lam.astype(jnp.float32))
    a = jnp.exp(log_a)
    mult = jnp.sqrt(-jnp.expm1(2.0 * log_a))
    if reset_first:
        mult = mult.at[:, 0].set(1.0)
    a_cum, b_cum = lax.associative_scan(_lru_combine, (a, mult * i * xf), axis=1)
    h = a_cum * h0.astype(jnp.float32)[:, None] + b_cum
    return h.astype(x.dtype), h[:, -1].astype(x.dtype)


def mixer_block(h, attend, lru_buf, lru_h0, sconv_buf, reset_first, w_in, sb_bias, lru_conv_w, lru_conv_b,
                lru_w_r, lru_b_r, lru_w_i, lru_b_i, lru_lambda, sconv_w, w_out_a, w_out_b, w_out_c, w_o):
    bn, t, _ = h.shape
    splits = (W_A, 2 * W_A, 3 * W_A, 3 * W_A + W_B, 3 * W_A + W_B + W_C,
              3 * W_A + W_B + 2 * W_C, 3 * W_A + W_B + 3 * W_C)
    proj = h @ w_in
    q, k, v, xb, xc, gb, gc, graw = jnp.split(proj, splits, axis=-1)
    q = q.reshape(bn, t, H_A, DH)
    k = k.reshape(bn, t, H_A, DH)
    v = v.reshape(bn, t, H_A, DH)
    y_a = attend(q, k, v, sb_bias).reshape(bn, t, W_A)
    xb_c, new_lru_buf = causal_conv(xb, lru_buf, lru_conv_w, lru_conv_b)
    y_b, h_last = rglru(xb_c, lru_h0, lru_w_r, lru_b_r, lru_w_i, lru_b_i, lru_lambda, reset_first)
    u_c, new_sconv = causal_conv(gc * xc, sconv_buf, sconv_w)
    y_c = gb * u_c
    gates = jax.nn.sigmoid(graw.astype(jnp.float32)).astype(h.dtype).reshape(bn, t, N_BRANCH, D_MODEL)
    merged = (gates[:, :, 0] * (y_a @ w_out_a) + gates[:, :, 1] * (y_b @ w_out_b)
              + gates[:, :, 2] * (y_c @ w_out_c))
    return merged @ w_o, k, v, new_lru_buf, h_last, new_sconv


def conv_ffn(h, buf, w_up, conv_w, conv_b, w_down):
    up = h @ w_up
    up_c, new_buf = causal_conv(up, buf, conv_w, conv_b)
    a, g = jnp.split(up_c, 2, axis=-1)
    return (jax.nn.gelu(g, approximate=True) * a) @ w_down, new_buf


def setup_inputs(seed: int = 0) -> dict:
    key = jax.random.key(seed)
    ks = jax.random.split(key, 40)
    f32 = jnp.float32

    def nrm(k, shape, scale=1.0):
        return jax.random.normal(k, shape, f32) * scale

    def gain(k, shape):
        return 1.0 + nrm(k, shape, 0.05)

    n_pages = PAST_LEN // PAGE_SIZE
    n_used = DEC_BATCH * n_pages
    n_pool = n_used + n_used // 4
    page_table = jax.random.permutation(ks[0], n_pool)[:n_used].reshape(DEC_BATCH, n_pages).astype(jnp.int32)
    p_a = jax.random.uniform(ks[1], (DEPTH, W_B), f32, 0.9, 0.999)
    return {
        'x_prompt': nrm(ks[2], (BATCH, SEQ, D_MODEL)),
        'x_sample': nrm(ks[3], (DEC_BATCH, DEC_SEQ, D_MODEL)),
        'cache_k': nrm(ks[4], (DEPTH, n_pool, PAGE_SIZE, H_A, DH)),
        'cache_v': nrm(ks[5], (DEPTH, n_pool, PAGE_SIZE, H_A, DH)),
        'state_lru_conv': nrm(ks[6], (DEPTH, DEC_BATCH, CONV_B - 1, W_B)),
        'state_lru_h': nrm(ks[7], (DEPTH, DEC_BATCH, W_B), 0.5),
        'state_sconv': nrm(ks[8], (DEPTH, DEC_BATCH, CONV_C - 1, W_C)),
        'state_ffn_conv': nrm(ks[9], (DEPTH, DEC_BATCH, CONV_F - 1, 2 * D_FF)),
        'page_table': page_table,
        'meta_tokens': nrm(ks[10], (N_META, D_MODEL)),
        'norm_mix_pre': gain(ks[11], (DEPTH, D_MODEL)),
        'norm_mix_post': gain(ks[12], (DEPTH, D_MODEL)),
        'norm_ffn_pre': gain(ks[13], (DEPTH, D_MODEL)),
        'norm_ffn_post': gain(ks[14], (DEPTH, D_MODEL)),
        'w_in': nrm(ks[15], (DEPTH, D_MODEL, N_IN), D_MODEL ** -0.5),
        'sb_bias': SB_BIAS_INIT + nrm(ks[31], (DEPTH, H_A), 0.1),
        'lru_conv_w': nrm(ks[16], (DEPTH, CONV_B, W_B), CONV_B ** -0.5),
        'lru_conv_b': nrm(ks[17], (DEPTH, W_B), 0.01),
        'lru_w_r': nrm(ks[18], (DEPTH, NB_LRU, BW_LRU, BW_LRU), BW_LRU ** -0.5),
        'lru_b_r': nrm(ks[19], (DEPTH, W_B), 0.01),
        'lru_w_i': nrm(ks[20], (DEPTH, NB_LRU, BW_LRU, BW_LRU), BW_LRU ** -0.5),
        'lru_b_i': nrm(ks[21], (DEPTH, W_B), 0.01),
        'lru_lambda': jnp.log(p_a) - jnp.log1p(-p_a),
        'sconv_w': nrm(ks[22], (DEPTH, CONV_C, W_C), CONV_C ** -0.5),
        'w_out_a': nrm(ks[23], (DEPTH, W_A, D_MODEL), W_A ** -0.5),
        'w_out_b': nrm(ks[24], (DEPTH, W_B, D_MODEL), W_B ** -0.5),
        'w_out_c': nrm(ks[25], (DEPTH, W_C, D_MODEL), W_C ** -0.5),
        'w_o': nrm(ks[26], (DEPTH, D_MODEL, D_MODEL), D_MODEL ** -0.5),
        'ffn_w_up': nrm(ks[27], (DEPTH, D_MODEL, 2 * D_FF), D_MODEL ** -0.5),
        'ffn_conv_w': nrm(ks[28], (DEPTH, CONV_F, 2 * D_FF), CONV_F ** -0.5),
        'ffn_conv_b': nrm(ks[29], (DEPTH, 2 * D_FF), 0.01),
        'ffn_w_down': nrm(ks[30], (DEPTH, D_FF, D_MODEL), D_FF ** -0.5),
    }


def reference(x_prompt, x_sample, cache_k, cache_v, state_lru_conv, state_lru_h, state_sconv,
              state_ffn_conv, page_table, meta_tokens, norm_mix_pre, norm_mix_post, norm_ffn_pre,
              norm_ffn_post, w_in, sb_bias, lru_conv_w, lru_conv_b, lru_w_r, lru_b_r, lru_w_i, lru_b_i,
              lru_lambda, sconv_w, w_out_a, w_out_b, w_out_c, w_o, ffn_w_up, ffn_conv_w, ffn_conv_b,
              ffn_w_down):

    def layer(x, l, attend, lru_buf, lru_h0, sconv_buf, ffn_buf, reset_first):
        m, k, v, lb, hl, sb = mixer_block(
            rmsnorm(x, norm_mix_pre[l]), attend, lru_buf, lru_h0, sconv_buf, reset_first,
            w_in[l], sb_bias[l], lru_conv_w[l], lru_conv_b[l], lru_w_r[l], lru_b_r[l], lru_w_i[l],
            lru_b_i[l], lru_lambda[l], sconv_w[l], w_out_a[l], w_out_b[l], w_out_c[l], w_o[l])
        x = x + rmsnorm(m, norm_mix_post[l])
        f, fb = conv_ffn(rmsnorm(x, norm_ffn_pre[l]), ffn_buf, ffn_w_up[l], ffn_conv_w[l],
                         ffn_conv_b[l], ffn_w_down[l])
        x = x + rmsnorm(f, norm_ffn_post[l])
        return x, k, v, lb, hl, sb, fb

    bp = x_prompt.shape[0]
    dt = x_prompt.dtype
    xp = jnp.concatenate([jnp.broadcast_to(meta_tokens.astype(dt)[None], (bp, N_META, D_MODEL)), x_prompt], axis=1)
    kp, vp, lbp, hlp, sbp, fbp = [], [], [], [], [], []
    for l in range(DEPTH):
        xp, k, v, lb, hl, sb, fb = layer(
            xp, l, prompt_attend,
            jnp.zeros((bp, CONV_B - 1, W_B), dt), jnp.zeros((bp, W_B), dt),
            jnp.zeros((bp, CONV_C - 1, W_C), dt), jnp.zeros((bp, CONV_F - 1, 2 * D_FF), dt), True)
        kp.append(k); vp.append(v); lbp.append(lb); hlp.append(hl); sbp.append(sb); fbp.append(fb)
    y_prompt = xp[:, N_META:]

    xs = x_sample
    db, n_pages = page_table.shape
    ks_, vs_, lbs, hls, sbs, fbs = [], [], [], [], [], []
    for l in range(DEPTH):
        past_k = cache_k[l][page_table].reshape(db, n_pages * PAGE_SIZE, H_A, DH)
        past_v = cache_v[l][page_table].reshape(db, n_pages * PAGE_SIZE, H_A, DH)
        attend = functools.partial(sample_attend, past_k=past_k, past_v=past_v)
        xs, k, v, lb, hl, sb, fb = layer(xs, l, attend, state_lru_conv[l], state_lru_h[l],
                                         state_sconv[l], state_ffn_conv[l], False)
        ks_.append(k); vs_.append(v); lbs.append(lb); hls.append(hl); sbs.append(sb); fbs.append(fb)
    y_sample = xs

    return (y_prompt, y_sample,
            jnp.stack(kp), jnp.stack(vp), jnp.stack(ks_), jnp.stack(vs_),
            jnp.stack(lbp), jnp.stack(lbs), jnp.stack(hlp), jnp.stack(hls),
            jnp.stack(sbp), jnp.stack(sbs), jnp.stack(fbp), jnp.stack(fbs))
```

```python
import functools

import jax
import jax.numpy as jnp
from jax import lax
from jax.experimental import pallas as pl
from jax.experimental.pallas import tpu as pltpu

F32 = jnp.float32
BF16 = jnp.bfloat16

D_MODEL = 2048
DEPTH = 2
N_META = 16
H_A = 8
DH = 128
W_A = H_A * DH
W_B = 512
NB_LRU = 8
W_C = 512
CONV_B = 4
CONV_C = 3
CONV_F = 3
D_FF = 5632
N_BRANCH = 3
LRU_C = 8.0
EPS = 1e-6
SCALE = DH ** -0.5
N_IN = 3 * W_A + W_B + 3 * W_C + N_BRANCH * D_MODEL
PAGE = 128

CB = 512
COL_XB, COL_XC, COL_GB, COL_GC = 6, 7, 8, 9
COL_GATE_A, COL_GATE_B, COL_GATE_C = 10, 14, 18

VMEM_LIMIT_BYTES = 58 * 1024 * 1024


def _cparams(n_axes):
    return pltpu.CompilerParams(dimension_semantics=("arbitrary",) * n_axes,
                                vmem_limit_bytes=VMEM_LIMIT_BYTES)


def _sigmoid(x):
    return 1.0 / (1.0 + jnp.exp(-x))


def _softplus(x):
    return jnp.maximum(x, 0.0) + jnp.log1p(jnp.exp(-jnp.abs(x)))


def _gelu_tanh(g):
    return 0.5 * g * (1.0 + jnp.tanh(0.7978845608028654 * (g + 0.044715 * (g * g * g))))


def _rms(x, g):
    return x * lax.rsqrt(jnp.mean(x * x, axis=-1, keepdims=True) + EPS) * g


def _rmsnorm_kernel(x_ref, g_ref, o_ref):
    o_ref[...] = _rms(x_ref[...], g_ref[...]).astype(o_ref.dtype)


def _rmsnorm_cast(x, g_all, l, tm):
    m = x.shape[0]
    return pl.pallas_call(
        _rmsnorm_kernel, grid=(m // tm,),
        in_specs=[pl.BlockSpec((tm, D_MODEL), lambda i: (i, 0)),
                  pl.BlockSpec((None, 1, D_MODEL), lambda i: (l, 0, 0))],
        out_specs=pl.BlockSpec((tm, D_MODEL), lambda i: (i, 0)),
        out_shape=jax.ShapeDtypeStruct((m, D_MODEL), BF16),
        compiler_params=_cparams(1))(x, g_all)


def _resid_norm_kernel(x_ref, m_ref, gpost_ref, gnext_ref, xo_ref, xn_ref):
    x1 = x_ref[...] + _rms(m_ref[...], gpost_ref[...])
    xo_ref[...] = x1
    xn_ref[...] = _rms(x1, gnext_ref[...]).astype(xn_ref.dtype)


def _resid_kernel(x_ref, m_ref, gpost_ref, xo_ref):
    xo_ref[...] = x_ref[...] + _rms(m_ref[...], gpost_ref[...])


def _resid_norm(x, m_in, gpost_all, l, gnext_all, l_next, tm):
    m = x.shape[0]
    row = pl.BlockSpec((tm, D_MODEL), lambda i: (i, 0))
    if gnext_all is None:
        return pl.pallas_call(
            _resid_kernel, grid=(m // tm,),
            in_specs=[row, row, pl.BlockSpec((None, 1, D_MODEL), lambda i: (l, 0, 0))],
            out_specs=row, out_shape=jax.ShapeDtypeStruct((m, D_MODEL), F32),
            compiler_params=_cparams(1))(x, m_in, gpost_all), None
    return pl.pallas_call(
        _resid_norm_kernel, grid=(m // tm,),
        in_specs=[row, row, pl.BlockSpec((None, 1, D_MODEL), lambda i: (l, 0, 0)),
                  pl.BlockSpec((None, 1, D_MODEL), lambda i: (l_next, 0, 0))],
        out_specs=[row, row],
        out_shape=[jax.ShapeDtypeStruct((m, D_MODEL), F32), jax.ShapeDtypeStruct((m, D_MODEL), BF16)],
        compiler_params=_cparams(1))(x, m_in, gpost_all, gnext_all)


def _mm_kernel(a_ref, w_ref, o_ref, wb_ref):
    @pl.when(pl.program_id(1) == 0)
    def _():
        wb_ref[...] = w_ref[...].astype(BF16)

    o_ref[...] = jnp.dot(a_ref[...], wb_ref[...], preferred_element_type=F32).astype(o_ref.dtype)


def _matmul(a, w_all, l, tm, tn, out_dtype=F32):
    m, k = a.shape
    n = w_all.shape[2]
    return pl.pallas_call(
        _mm_kernel, grid=(n // tn, m // tm),
        in_specs=[pl.BlockSpec((tm, k), lambda j, i: (i, 0)),
                  pl.BlockSpec((None, k, tn), lambda j, i: (l, 0, j))],
        out_specs=pl.BlockSpec((tm, tn), lambda j, i: (i, j)),
        out_shape=jax.ShapeDtypeStruct((m, n), out_dtype),
        scratch_shapes=[pltpu.VMEM((k, tn), BF16)],
        compiler_params=_cparams(2))(a, w_all)


def _merge_kernel(ya_ref, yb_ref, yc_ref, ga_ref, gb_ref, gc_ref, wa_ref, wb_ref, wc_ref, o_ref,
                  wab, wbb, wcb):
    @pl.when(pl.program_id(1) == 0)
    def _():
        wab[...] = wa_ref[...].astype(BF16)
        wbb[...] = wb_ref[...].astype(BF16)
        wcb[...] = wc_ref[...].astype(BF16)

    acc = _sigmoid(ga_ref[...]) * jnp.dot(ya_ref[...], wab[...], preferred_element_type=F32)
    acc = acc + _sigmoid(gb_ref[...]) * jnp.dot(yb_ref[...], wbb[...], preferred_element_type=F32)
    acc = acc + _sigmoid(gc_ref[...]) * jnp.dot(yc_ref[...], wcb[...], preferred_element_type=F32)
    o_ref[...] = acc.astype(o_ref.dtype)


def _merge(ya, yb, yc, proj, wa_all, wb_all, wc_all, l, tm):
    m = ya.shape[0]
    tn = CB
    return pl.pallas_call(
        _merge_kernel, grid=(D_MODEL // tn, m // tm),
        in_specs=[pl.BlockSpec((tm, W_A), lambda j, i: (i, 0)),
                  pl.BlockSpec((tm, W_B), lambda j, i: (i, 0)),
                  pl.BlockSpec((tm, W_C), lambda j, i: (i, 0)),
                  pl.BlockSpec((tm, tn), lambda j, i: (i, COL_GATE_A + j)),
                  pl.BlockSpec((tm, tn), lambda j, i: (i, COL_GATE_B + j)),
                  pl.BlockSpec((tm, tn), lambda j, i: (i, COL_GATE_C + j)),
                  pl.BlockSpec((None, W_A, tn), lambda j, i: (l, 0, j)),
                  pl.BlockSpec((None, W_B, tn), lambda j, i: (l, 0, j)),
                  pl.BlockSpec((None, W_C, tn), lambda j, i: (l, 0, j))],
        out_specs=pl.BlockSpec((tm, tn), lambda j, i: (i, j)),
        out_shape=jax.ShapeDtypeStruct((m, D_MODEL), BF16),
        scratch_shapes=[pltpu.VMEM((W_A, tn), BF16), pltpu.VMEM((W_B, tn), BF16),
                        pltpu.VMEM((W_C, tn), BF16)],
        compiler_params=_cparams(2))(ya, yb, yc, proj, proj, proj, wa_all, wb_all, wc_all)


def _ffn_up_kernel(*refs, tm, shift, halo, tiles_per_seq, has_init):
    if has_init:
        (a_ref, wa_ref, wg_ref, cwa_ref, cwg_ref, cba_ref, cbg_ref, ia_ref, ig_ref,
         act_ref, sta_ref, stg_ref, wab, wgb, ua, ug) = refs
    else:
        (a_ref, wa_ref, wg_ref, cwa_ref, cwg_ref, cba_ref, cbg_ref,
         act_ref, sta_ref, stg_ref, wab, wgb, ua, ug) = refs
    i = pl.program_id(1)
    c = i % tiles_per_seq

    @pl.when(i == 0)
    def _():
        wab[...] = wa_ref[...].astype(BF16)
        wgb[...] = wg_ref[...].astype(BF16)

    if has_init:
        ua[0:halo, :] = ia_ref[...]
        ug[0:halo, :] = ig_ref[...]
    else:
        @pl.when(c == 0)
        def _():
            ua[0:halo, :] = jnp.zeros((halo, ua.shape[1]), F32)
            ug[0:halo, :] = jnp.zeros((halo, ug.shape[1]), F32)

        @pl.when(c > 0)
        def _():
            ua[0:halo, :] = ua[tm:tm + halo, :]
            ug[0:halo, :] = ug[tm:tm + halo, :]

    a = a_ref[...]
    ua[halo:halo + tm, :] = jnp.dot(a, wab[...], preferred_element_type=F32)
    ug[halo:halo + tm, :] = jnp.dot(a, wgb[...], preferred_element_type=F32)

    def conv(u, cw_ref, cb_ref):
        cw = cw_ref[...]
        y = u[halo - 2 * shift:halo - 2 * shift + tm, :] * cw[0:1, :]
        y = y + u[halo - shift:halo - shift + tm, :] * cw[1:2, :]
        y = y + u[halo:halo + tm, :] * cw[2:3, :]
        return y + cb_ref[...]

    act = _gelu_tanh(conv(ug, cwg_ref, cbg_ref)) * conv(ua, cwa_ref, cba_ref)
    act_ref[...] = act.astype(act_ref.dtype)

    ns = (CONV_F - 1) * shift

    @pl.when(c == tiles_per_seq - 1)
    def _():
        sta_ref[...] = ua[halo + tm - ns:halo + tm, :]
        stg_ref[...] = ug[halo + tm - ns:halo + tm, :]


def _ffn_up(a, w_up_all, cw_all, cb_all, l, *, tm, shift, halo, tiles_per_seq, init=None):
    m = a.shape[0]
    tn = CB
    nj = D_FF // tn
    n_seq_tiles = m // tm
    has_init = init is not None
    ns = (CONV_F - 1) * shift
    in_specs = [pl.BlockSpec((tm, D_MODEL), lambda j, i: (i, 0)),
                pl.BlockSpec((None, D_MODEL, tn), lambda j, i: (l, 0, j)),
                pl.BlockSpec((None, D_MODEL, tn), lambda j, i: (l, 0, nj + j)),
                pl.BlockSpec((None, CONV_F, tn), lambda j, i: (l, 0, j)),
                pl.BlockSpec((None, CONV_F, tn), lambda j, i: (l, 0, nj + j)),
                pl.BlockSpec((None, 1, tn), lambda j, i: (l, 0, j)),
                pl.BlockSpec((None, 1, tn), lambda j, i: (l, 0, nj + j))]
    args = [a, w_up_all, w_up_all, cw_all, cw_all, cb_all, cb_all]
    if has_init:
        in_specs += [pl.BlockSpec((halo, tn), lambda j, i: (0, j)),
                     pl.BlockSpec((halo, tn), lambda j, i: (0, nj + j))]
        args += [init, init]
        st_shape = jax.ShapeDtypeStruct((ns, D_FF), F32)
        st_spec = pl.BlockSpec((ns, tn), lambda j, i: (0, j))
    else:
        n_seq = n_seq_tiles // tiles_per_seq
        st_shape = jax.ShapeDtypeStruct((n_seq, ns, D_FF), F32)
        st_spec = pl.BlockSpec((None, ns, tn), lambda j, i: (i // tiles_per_seq, 0, j))
    kern = functools.partial(_ffn_up_kernel, tm=tm, shift=shift, halo=halo,
                             tiles_per_seq=tiles_per_seq, has_init=has_init)
    return pl.pallas_call(
        kern, grid=(nj, n_seq_tiles),
        in_specs=in_specs,
        out_specs=[pl.BlockSpec((tm, tn), lambda j, i: (i, j)), st_spec, st_spec],
        out_shape=[jax.ShapeDtypeStruct((m, D_FF), BF16), st_shape, st_shape],
        scratch_shapes=[pltpu.VMEM((D_MODEL, tn), BF16), pltpu.VMEM((D_MODEL, tn), BF16),
                        pltpu.VMEM((halo + tm, tn), F32), pltpu.VMEM((halo + tm, tn), F32)],
        compiler_params=_cparams(2))(*args)


def _lru_coeffs(xbc, wri_ref, br_ref, bi_ref, lam_ref):
    ri = jnp.dot(xbc.astype(BF16), wri_ref[...], preferred_element_type=F32)
    r = _sigmoid(ri[:, :W_B] + br_ref[...])
    ig = _sigmoid(ri[:, W_B:] + bi_ref[...])
    log_a = (-LRU_C) * r * _softplus(-lam_ref[...])
    a = jnp.exp(log_a)
    mult = jnp.sqrt(1.0 - jnp.exp(2.0 * log_a))
    return a, mult, ig * xbc


def _bc_prompt_kernel(xb_ref, xc_ref, gb_ref, gc_ref, cw_ref, cb_ref, wri_ref, br_ref, bi_ref,
                      lam_ref, sw_ref, yb_ref, yc_ref, lbuf_ref, hlast_ref, sbuf_ref,
                      xs, us, a_s, b_s, h_s, hprev, *, tm, tiles_per_seq):
    c = pl.program_id(1)
    first = c == 0

    @pl.when(first)
    def _():
        xs[0:8, :] = jnp.zeros((8, W_B), F32)
        us[0:8, :] = jnp.zeros((8, W_C), F32)
        hprev[...] = jnp.zeros((8, W_B), F32)

    @pl.when(c > 0)
    def _():
        xs[0:8, :] = xs[tm:tm + 8, :]
        us[0:8, :] = us[tm:tm + 8, :]

    xs[8:8 + tm, :] = xb_ref[...]
    us[8:8 + tm, :] = gc_ref[...] * xc_ref[...]

    cw = cw_ref[...]
    xbc = cb_ref[...] + xs[8:8 + tm, :] * cw[3:4, :]
    for i in range(CONV_B - 1):
        xbc = xbc + xs[5 + i:5 + i + tm, :] * cw[i:i + 1, :]
    a, mult, ix = _lru_coeffs(xbc, wri_ref, br_ref, bi_ref, lam_ref)
    a_s[...] = a
    b_s[...] = mult * ix

    @pl.when(first)
    def _():
        b_s[0:1, :] = ix[0:1, :]

    row8 = lax.broadcasted_iota(jnp.int32, (8, W_B), 0)

    def tile_body(t, hp):
        off = pl.multiple_of(t * 8, 8)
        av = a_s[pl.ds(off, 8), :]
        bv = b_s[pl.ds(off, 8), :]
        for k in (1, 2, 4):
            a_sh = jnp.where(row8 >= k, pltpu.roll(av, k, 0), 1.0)
            b_sh = jnp.where(row8 >= k, pltpu.roll(bv, k, 0), 0.0)
            bv = av * b_sh + bv
            av = av * a_sh
        hv = av * hp + bv
        h_s[pl.ds(off, 8), :] = hv
        return jnp.broadcast_to(hv[7:8, :], (8, W_B))

    hprev[...] = lax.fori_loop(0, tm // 8, tile_body, hprev[...])
    yb_ref[...] = h_s[...].astype(yb_ref.dtype)

    sw = sw_ref[...]
    yc = us[6:6 + tm, :] * sw[0:1, :] + us[7:7 + tm, :] * sw[1:2, :] + us[8:8 + tm, :] * sw[2:3, :]
    yc_ref[...] = (gb_ref[...] * yc).astype(yc_ref.dtype)

    @pl.when(c == tiles_per_seq - 1)
    def _():
        lbuf_ref[...] = xs[8 + tm - (CONV_B - 1):8 + tm, :]
        hlast_ref[...] = h_s[tm - 1:tm, :]
        sbuf_ref[...] = us[8 + tm - (CONV_C - 1):8 + tm, :]


def _bc_prompt(proj, n_seq, tm, tiles_per_seq, cw_all, cb_all, wri_all, br_all, bi_all, lam_all,
               sw_all, l):
    m = proj.shape[0]
    col = lambda cidx: pl.BlockSpec((tm, CB), lambda b, c: (b * tiles_per_seq + c, cidx))
    par = lambda rows, width: pl.BlockSpec((None, rows, width), lambda b, c: (l, 0, 0))
    seq_out = lambda rows: pl.BlockSpec((None, rows, CB), lambda b, c: (b, 0, 0))
    row_out = pl.BlockSpec((tm, CB), lambda b, c: (b * tiles_per_seq + c, 0))
    kern = functools.partial(_bc_prompt_kernel, tm=tm, tiles_per_seq=tiles_per_seq)
    return pl.pallas_call(
        kern, grid=(n_seq, tiles_per_seq),
        in_specs=[col(COL_XB), col(COL_XC), col(COL_GB), col(COL_GC),
                  par(CONV_B, W_B), par(1, W_B), par(W_B, 2 * W_B), par(1, W_B), par(1, W_B),
                  par(1, W_B), par(CONV_C, W_C)],
        out_specs=[row_out, row_out, seq_out(CONV_B - 1), seq_out(1), seq_out(CONV_C - 1)],
        out_shape=[jax.ShapeDtypeStruct((m, W_B), BF16), jax.ShapeDtypeStruct((m, W_C), BF16),
                   jax.ShapeDtypeStruct((n_seq, CONV_B - 1, W_B), F32),
                   jax.ShapeDtypeStruct((n_seq, 1, W_B), F32),
                   jax.ShapeDtypeStruct((n_seq, CONV_C - 1, W_C), F32)],
        scratch_shapes=[pltpu.VMEM((8 + tm, W_B), F32), pltpu.VMEM((8 + tm, W_C), F32),
                        pltpu.VMEM((tm, W_B), F32), pltpu.VMEM((tm, W_B), F32),
                        pltpu.VMEM((tm, W_B), F32), pltpu.VMEM((8, W_B), F32)],
        compiler_params=_cparams(2))(proj, proj, proj, proj, cw_all, cb_all, wri_all, br_all,
                                     bi_all, lam_all, sw_all)


def _bc_sample_kernel(xb_ref, xc_ref, gb_ref, gc_ref, lbuf_in, h0_ref, sbuf_in, cw_ref, cb_ref,
                      wri_ref, br_ref, bi_ref, lam_ref, sw_ref, yb_ref, yc_ref, lbuf_ref, hlast_ref,
                      sbuf_ref, xs, us, *, nb, nt):
    m = nb * nt
    hb = (CONV_B - 1) * nb
    hc = (CONV_C - 1) * nb
    xs[0:hb, :] = lbuf_in[...]
    xs[hb:hb + m, :] = xb_ref[...]
    us[0:hc, :] = sbuf_in[...]
    us[hc:hc + m, :] = gc_ref[...] * xc_ref[...]

    cw = cw_ref[...]
    xbc = cb_ref[...] + xs[hb:hb + m, :] * cw[3:4, :]
    for i in range(CONV_B - 1):
        xbc = xbc + xs[i * nb:i * nb + m, :] * cw[i:i + 1, :]
    a, mult, ix = _lru_coeffs(xbc, wri_ref, br_ref, bi_ref, lam_ref)
    bterm = mult * ix
    h = h0_ref[...]
    for t in range(nt):
        h = a[t * nb:(t + 1) * nb, :] * h + bterm[t * nb:(t + 1) * nb, :]
        yb_ref[t * nb:(t + 1) * nb, :] = h.astype(yb_ref.dtype)
    hlast_ref[...] = h
    lbuf_ref[...] = xs[m:m + hb, :]

    sw = sw_ref[...]
    yc = us[0:m, :] * sw[0:1, :] + us[nb:nb + m, :] * sw[1:2, :] + us[2 * nb:2 * nb + m, :] * sw[2:3, :]
    yc_ref[...] = (gb_ref[...] * yc).astype(yc_ref.dtype)
    sbuf_ref[...] = us[m:m + hc, :]


def _bc_sample(proj, nb, nt, lbuf, h0, sbuf, cw_all, cb_all, wri_all, br_all, bi_all, lam_all,
               sw_all, l):
    m = nb * nt
    hb = (CONV_B - 1) * nb
    hc = (CONV_C - 1) * nb
    col = lambda cidx: pl.BlockSpec((m, CB), lambda i: (0, cidx))
    full = lambda rows: pl.BlockSpec((rows, CB), lambda i: (0, 0))
    par = lambda rows, width: pl.BlockSpec((None, rows, width), lambda i: (l, 0, 0))
    kern = functools.partial(_bc_sample_kernel, nb=nb, nt=nt)
    return pl.pallas_call(
        kern, grid=(1,),
        in_specs=[col(COL_XB), col(COL_XC), col(COL_GB), col(COL_GC), full(hb), full(nb), full(hc),
                  par(CONV_B, W_B), par(1, W_B), par(W_B, 2 * W_B), par(1, W_B), par(1, W_B),
                  par(1, W_B), par(CONV_C, W_C)],
        out_specs=[full(m), full(m), full(hb), full(nb), full(hc)],
        out_shape=[jax.ShapeDtypeStruct((m, W_B), BF16), jax.ShapeDtypeStruct((m, W_C), BF16),
                   jax.ShapeDtypeStruct((hb, W_B), F32), jax.ShapeDtypeStruct((nb, W_B), F32),
                   jax.ShapeDtypeStruct((hc, W_C), F32)],
        scratch_shapes=[pltpu.VMEM((hb + m, W_B), F32), pltpu.VMEM((hc + m, W_C), F32)],
        compiler_params=_cparams(1))(proj, proj, proj, proj, lbuf, h0, sbuf, cw_all, cb_all,
                                     wri_all, br_all, bi_all, lam_all, sw_all)


def _split_bf16(x):
    hi = x.astype(BF16)
    lo = (x - hi.astype(F32)).astype(BF16)
    return hi, lo


def _sb_block(qt, kblk, vblk, tri, bias, carry, acc, mask):
    z = lax.dot_general(qt, kblk, (((1,), (1,)), ((), ())), preferred_element_type=F32) * SCALE + bias
    sp = _softplus(z)
    lk = -sp if mask is None else jnp.where(mask, -sp, 0.0)
    hi, lo = _split_bf16(lk)
    btw = (jnp.dot(hi, tri, preferred_element_type=F32) + jnp.dot(lo, tri, preferred_element_type=F32)
           + carry)
    w = jnp.exp(z - sp + btw)
    if mask is not None:
        w = jnp.where(mask, w, 0.0)
    acc = acc + jnp.dot(w.astype(BF16), vblk, preferred_element_type=F32)
    carry = carry + jnp.sum(lk, axis=1, keepdims=True)
    return acc, carry


def _attn_prompt_kernel(bias_ref, q_ref, k_ref, v_ref, tri_q_ref, tri_m_ref, o_ref, qb, kb, vb, *,
                        layer, tq, n_real):
    bias = bias_ref[layer, pl.program_id(1)]
    qb[...] = q_ref[...].astype(BF16)
    kb[...] = k_ref[...].astype(BF16)
    vb[...] = v_ref[...].astype(BF16)
    tri_q = tri_q_ref[...]
    tri_m = tri_m_ref[...]
    diag_mask = (lax.broadcasted_iota(jnp.int32, (tq, tq), 1)
                 < lax.broadcasted_iota(jnp.int32, (tq, tq), 0))
    meta_mask = lax.broadcasted_iota(jnp.int32, (tq, PAGE), 1) < N_META

    for qi in range(n_real // tq):
        q0 = N_META + tq * qi
        qt = qb[q0:q0 + tq, :]
        acc = jnp.zeros((tq, DH), F32)
        carry = jnp.zeros((tq, 1), F32)
        acc, carry = _sb_block(qt, kb[q0:q0 + tq, :], vb[q0:q0 + tq, :], tri_q, bias, carry, acc,
                               diag_mask)
        if qi > 0:
            def body(jj, st, qi=qi, qt=qt):
                off = pl.multiple_of(N_META + (qi - 1 - jj) * tq, 16)
                return _sb_block(qt, kb[pl.ds(off, tq), :], vb[pl.ds(off, tq), :], tri_q, bias,
                                 st[1], st[0], None)

            acc, carry = lax.fori_loop(0, qi, body, (acc, carry))
        acc, carry = _sb_block(qt, kb[0:PAGE, :], vb[0:PAGE, :], tri_m, bias, carry, acc, meta_mask)
        o_ref[q0:q0 + tq, :] = acc.astype(o_ref.dtype)

    mm = (lax.broadcasted_iota(jnp.int32, (N_META, PAGE), 1)
          < lax.broadcasted_iota(jnp.int32, (N_META, PAGE), 0))
    acc, _ = _sb_block(qb[0:N_META, :], kb[0:PAGE, :], vb[0:PAGE, :], tri_m, bias,
                       jnp.zeros((N_META, 1), F32), jnp.zeros((N_META, DH), F32), mm)
    o_ref[0:N_META, :] = acc.astype(o_ref.dtype)


def _later_key_matrix(n):
    return (lax.broadcasted_iota(jnp.int32, (n, n), 0)
            > lax.broadcasted_iota(jnp.int32, (n, n), 1)).astype(BF16)


def _attn_prompt(proj, sb_bias, l, n_seq, t_seq, tq):
    m = proj.shape[0]
    kern = functools.partial(_attn_prompt_kernel, layer=l, tq=tq, n_real=t_seq - N_META)
    head = lambda off: pl.BlockSpec((t_seq, DH), lambda b, h: (b, off + h))
    const = lambda n: pl.BlockSpec((n, n), lambda b, h: (0, 0))
    return pl.pallas_call(
        kern, grid=(n_seq, H_A),
        in_specs=[pl.BlockSpec(memory_space=pltpu.SMEM), head(0), head(H_A), head(2 * H_A),
                  const(tq), const(PAGE)],
        out_specs=pl.BlockSpec((t_seq, DH), lambda b, h: (b, h)),
        out_shape=jax.ShapeDtypeStruct((m, W_A), BF16),
        scratch_shapes=[pltpu.VMEM((t_seq, DH), BF16)] * 3,
        compiler_params=_cparams(2))(sb_bias, proj, proj, proj, _later_key_matrix(tq),
                                     _later_key_matrix(PAGE))


def _attn_sample_kernel(pt_ref, q_ref, kn_ref, vn_ref, k0_ref, k1_ref, v0_ref, v1_ref, bias_ref,
                        tri_ref, o_ref, qf, qbd, kpad, vpad, acc, carry, *, nt, n_pages):
    del pt_ref
    j = pl.program_id(1)
    lanes = 2 * H_A * nt
    wk = 2 * W_A

    @pl.when(j == 0)
    def _():
        qf[...] = jnp.zeros((lanes, wk), F32)
        kpad[...] = jnp.zeros((PAGE, wk), F32)
        vpad[...] = jnp.zeros((PAGE, wk), F32)
        for b in range(2):
            kpad[0:nt, b * W_A:(b + 1) * W_A] = kn_ref[b]
            vpad[0:nt, b * W_A:(b + 1) * W_A] = vn_ref[b]
            for h in range(H_A):
                r0 = (b * H_A + h) * nt
                c0 = b * W_A + h * DH
                qf[r0:r0 + nt, c0:c0 + DH] = q_ref[b, :, h * DH:(h + 1) * DH]
        qbd[...] = qf[...].astype(BF16)
        acc[...] = jnp.zeros((lanes, wk), F32)
        carry[...] = jnp.zeros((1, lanes), F32)

    def step(k2, v2, mask):
        z = lax.dot_general(k2.astype(BF16), qbd[...], (((1,), (1,)), ((), ())),
                            preferred_element_type=F32) * SCALE + bias_ref[...]
        sp = _softplus(z)
        lk = -sp if mask is None else jnp.where(mask, -sp, 0.0)
        hi, lo = _split_bf16(lk)
        bt = jnp.dot(tri_ref[...], jnp.concatenate([hi, lo], axis=1), preferred_element_type=F32)
        w = jnp.exp(z - sp + bt[:, :lanes] + bt[:, lanes:] + carry[...])
        if mask is not None:
            w = jnp.where(mask, w, 0.0)
        acc[...] += jnp.dot(w.T.astype(BF16), v2.astype(BF16), preferred_element_type=F32)
        carry[...] += jnp.sum(lk, axis=0, keepdims=True)

    @pl.when(j == 0)
    def _():
        row = lax.broadcasted_iota(jnp.int32, (PAGE, lanes), 0)
        tq = lax.broadcasted_iota(jnp.int32, (PAGE, lanes), 1) % nt
        step(kpad[...], vpad[...], row < tq)

    @pl.when(j > 0)
    def _():
        step(jnp.concatenate([k0_ref[...], k1_ref[...]], axis=1),
             jnp.concatenate([v0_ref[...], v1_ref[...]], axis=1), None)

    @pl.when(j == n_pages)
    def _():
        for b in range(2):
            for h in range(H_A):
                r0 = (b * H_A + h) * nt
                c0 = b * W_A + h * DH
                o_ref[b, :, h * DH:(h + 1) * DH] = acc[r0:r0 + nt, c0:c0 + DH]


def _attn_sample(q, kn, vn, cache_k, cache_v, page_table, bias_lanes, l):
    nb, nt, _ = q.shape
    n_pages = page_table.shape[1]
    pt = page_table.reshape(-1)
    lanes = 2 * H_A * nt
    wk = 2 * W_A
    tok = pl.BlockSpec((2, nt, W_A), lambda p, j, pt_ref: (p, 0, 0))

    def page(which):
        def index_map(p, j, pt_ref):
            pg = jnp.minimum(n_pages - 1, n_pages - j)
            return (l, pt_ref[(2 * p + which) * n_pages + pg], 0, 0)
        return pl.BlockSpec((None, None, PAGE, W_A), index_map)

    kern = functools.partial(_attn_sample_kernel, nt=nt, n_pages=n_pages)
    grid_spec = pltpu.PrefetchScalarGridSpec(
        num_scalar_prefetch=1, grid=(nb // 2, n_pages + 1),
        in_specs=[tok, tok, tok, page(0), page(1), page(0), page(1),
                  pl.BlockSpec((1, lanes), lambda p, j, pt_ref: (0, 0)),
                  pl.BlockSpec((PAGE, PAGE), lambda p, j, pt_ref: (0, 0))],
        out_specs=tok,
        scratch_shapes=[pltpu.VMEM((lanes, wk), F32), pltpu.VMEM((lanes, wk), BF16),
                        pltpu.VMEM((PAGE, wk), F32), pltpu.VMEM((PAGE, wk), F32),
                        pltpu.VMEM((lanes, wk), F32), pltpu.VMEM((1, lanes), F32)])
    tri = _later_key_matrix(PAGE).T
    return pl.pallas_call(
        kern, grid_spec=grid_spec, out_shape=jax.ShapeDtypeStruct((nb, nt, W_A), F32),
        compiler_params=_cparams(2))(pt, q, kn, vn, cache_k, cache_k, cache_v, cache_v, bias_lanes,
                                     tri)


def _block_diag(w):
    depth, nblk, c, d = w.shape
    eye = jnp.eye(nblk, dtype=w.dtype)
    return jnp.einsum('lncd,nm->lncmd', w, eye).reshape(depth, nblk * c, nblk * d)


def kernel(x_prompt, x_sample, cache_k, cache_v, state_lru_conv, state_lru_h, state_sconv, state_ffn_conv, page_table, meta_tokens, norm_mix_pre, norm_mix_post, norm_ffn_pre, norm_ffn_post, w_in, sb_bias, lru_conv_w, lru_conv_b, lru_w_r, lru_b_r, lru_w_i, lru_b_i, lru_lambda, sconv_w, w_out_a, w_out_b, w_out_c, w_o, ffn_w_up, ffn_conv_w, ffn_conv_b, ffn_w_down):
    bp, seq, _ = x_prompt.shape
    nb, nt, _ = x_sample.shape
    t_seq = N_META + seq
    n_pool = cache_k.shape[1]

    row3 = lambda p: p.reshape(DEPTH, 1, p.shape[-1])
    g_mix_pre, g_mix_post = row3(norm_mix_pre), row3(norm_mix_post)
    g_ffn_pre, g_ffn_post = row3(norm_ffn_pre), row3(norm_ffn_post)
    lcb, lbr, lbi, lam = row3(lru_conv_b), row3(lru_b_r), row3(lru_b_i), row3(lru_lambda)
    fcb = row3(ffn_conv_b)
    wri = jnp.concatenate([_block_diag(lru_w_r), _block_diag(lru_w_i)], axis=-1).astype(BF16)
    ck = cache_k.reshape(DEPTH, n_pool, PAGE, W_A)
    cv = cache_v.reshape(DEPTH, n_pool, PAGE, W_A)

    tp = t_seq // 3
    ms = nb * nt

    def layer(x, xn, l, prompt):
        last = l == DEPTH - 1
        if prompt:
            tm_big, tm, tm_row = t_seq, tp, tp
        else:
            tm_big, tm, tm_row = ms, ms, ms // 4
        proj = _matmul(xn, w_in, l, tm_big, CB)
        k = proj[:, W_A:2 * W_A]
        v = proj[:, 2 * W_A:3 * W_A]
        if prompt:
            ya = _attn_prompt(proj, sb_bias, l, bp, t_seq, 256)
            yb, yc, lbuf, hlast, sbuf = _bc_prompt(proj, bp, tm, 3, lru_conv_w, lcb, wri, lbr, lbi,
                                                   lam, sconv_w, l)
            hlast = hlast.reshape(bp, W_B)
            k_out = k.reshape(bp, t_seq, H_A, DH)
            v_out = v.reshape(bp, t_seq, H_A, DH)
        else:
            to_bm = lambda a: a.reshape(-1, nb, a.shape[-1]).transpose(1, 0, 2)
            to_tm = lambda a: a.transpose(1, 0, 2).reshape(a.shape[1] * nb, a.shape[-1])
            q_bm, k_bm, v_bm = to_bm(proj[:, :W_A]), to_bm(k), to_bm(v)
            bias_lanes = jnp.tile(jnp.repeat(sb_bias[l], nt), 2).reshape(1, 2 * H_A * nt)
            ya = to_tm(_attn_sample(q_bm, k_bm, v_bm, ck, cv, page_table, bias_lanes, l)).astype(BF16)
            yb, yc, lbuf, hlast, sbuf = _bc_sample(
                proj, nb, nt, to_tm(state_lru_conv[l]), state_lru_h[l], to_tm(state_sconv[l]),
                lru_conv_w, lcb, wri, lbr, lbi, lam, sconv_w, l)
            lbuf, sbuf = to_bm(lbuf), to_bm(sbuf)
            k_out = k_bm.reshape(nb, nt, H_A, DH)
            v_out = v_bm.reshape(nb, nt, H_A, DH)
        merged = _merge(ya, yb, yc, proj, w_out_a, w_out_b, w_out_c, l, tm)
        mo = _matmul(merged, w_o, l, tm_big, CB)
        x, xn = _resid_norm(x, mo, g_mix_post, l, g_ffn_pre, l, tm_row)
        if prompt:
            act, sta, stg = _ffn_up(xn, ffn_w_up, ffn_conv_w, fcb, l, tm=tm, shift=1, halo=8,
                                    tiles_per_seq=3)
            fbuf = jnp.concatenate([sta, stg], axis=-1)
        else:
            act, sta, stg = _ffn_up(xn, ffn_w_up, ffn_conv_w, fcb, l, tm=tm, shift=nb,
                                    halo=(CONV_F - 1) * nb, tiles_per_seq=1,
                                    init=to_tm(state_ffn_conv[l]))
            fbuf = to_bm(jnp.concatenate([sta, stg], axis=-1))
        f = _matmul(act, ffn_w_down, l, tm, 256)
        x, xn = _resid_norm(x, f, g_ffn_post, l, None if last else g_mix_pre, l + 1, tm_row)
        return x, xn, (k_out, v_out, lbuf, hlast, sbuf, fbuf)

    def run(x, prompt, tm):
        xn = _rmsnorm_cast(x, g_mix_pre, 0, tm)
        outs = []
        for l in range(DEPTH):
            x, xn, o = layer(x, xn, l, prompt)
            outs.append(o)
        return x, [jnp.stack(z) for z in zip(*outs)]

    dt = x_prompt.dtype
    xp = jnp.concatenate([jnp.broadcast_to(meta_tokens.astype(dt)[None], (bp, N_META, D_MODEL)),
                          x_prompt], axis=1).reshape(bp * t_seq, D_MODEL)
    xp, (kp, vp, lbp, hlp, sbp, fbp) = run(xp, True, tp)
    y_prompt = xp.reshape(bp, t_seq, D_MODEL)[:, N_META:]

    xs = x_sample.transpose(1, 0, 2).reshape(ms, D_MODEL)
    xs, (ks, vs, lbs, hls, sbs, fbs) = run(xs, False, ms // 4)
    y_sample = xs.reshape(nt, nb, D_MODEL).transpose(1, 0, 2)

    return (y_prompt, y_sample, kp, vp, ks, vs, lbp, lbs, hlp, hls, sbp, sbs, fbp, fbs)
```

```python
import functools

import jax
import jax.numpy as jnp
from jax import lax
from jax.experimental import pallas as pl
from jax.experimental.pallas import tpu as pltpu

F32 = jnp.float32
BF16 = jnp.bfloat16

D_MODEL = 2048
DEPTH = 2
N_META = 16
H_A = 8
DH = 128
W_A = H_A * DH
W_B = 512
NB_LRU = 8
W_C = 512
CONV_B = 4
CONV_C = 3
CONV_F = 3
D_FF = 5632
N_BRANCH = 3
LRU_C = 8.0
EPS = 1e-6
SCALE = DH ** -0.5
N_IN = 3 * W_A + W_B + 3 * W_C + N_BRANCH * D_MODEL
PAGE = 128

CB = 512
COL_XB, COL_XC, COL_GB, COL_GC = 0, 1, 2, 3
COL_GATE_A, COL_GATE_B, COL_GATE_C = 4, 8, 12

VMEM_LIMIT_BYTES = 58 * 1024 * 1024


def _cparams(n_axes):
    return pltpu.CompilerParams(dimension_semantics=("arbitrary",) * n_axes,
                                vmem_limit_bytes=VMEM_LIMIT_BYTES)


def _sigmoid(x):
    return 1.0 / (1.0 + jnp.exp(-x))


def _softplus(x):
    return jnp.maximum(x, 0.0) + jnp.log(1.0 + jnp.exp(-jnp.abs(x)))


def _gelu_tanh(g):
    return 0.5 * g * (1.0 + jnp.tanh(0.7978845608028654 * (g + 0.044715 * (g * g * g))))


def _rms(x, g):
    return x * lax.rsqrt(jnp.mean(x * x, axis=-1, keepdims=True) + EPS) * g


def _rmsnorm_kernel(x_ref, g_ref, o_ref):
    o_ref[...] = _rms(x_ref[...], g_ref[...]).astype(o_ref.dtype)


def _rmsnorm_cast(x, g_all, l, tm):
    m = x.shape[0]
    return pl.pallas_call(
        _rmsnorm_kernel, grid=(m // tm,),
        in_specs=[pl.BlockSpec((tm, D_MODEL), lambda i: (i, 0)),
                  pl.BlockSpec((None, 1, D_MODEL), lambda i: (l, 0, 0))],
        out_specs=pl.BlockSpec((tm, D_MODEL), lambda i: (i, 0)),
        out_shape=jax.ShapeDtypeStruct((m, D_MODEL), BF16),
        compiler_params=_cparams(1))(x, g_all)


def _resid_norm_kernel(x_ref, m_ref, gpost_ref, gnext_ref, xo_ref, xn_ref):
    x1 = x_ref[...] + _rms(m_ref[...], gpost_ref[...])
    xo_ref[...] = x1
    xn_ref[...] = _rms(x1, gnext_ref[...]).astype(xn_ref.dtype)


def _resid_kernel(x_ref, m_ref, gpost_ref, xo_ref):
    xo_ref[...] = x_ref[...] + _rms(m_ref[...], gpost_ref[...])


def _resid_norm(x, m_in, gpost_all, l, gnext_all, l_next, tm):
    m = x.shape[0]
    row = pl.BlockSpec((tm, D_MODEL), lambda i: (i, 0))
    if gnext_all is None:
        return pl.pallas_call(
            _resid_kernel, grid=(m // tm,),
            in_specs=[row, row, pl.BlockSpec((None, 1, D_MODEL), lambda i: (l, 0, 0))],
            out_specs=row, out_shape=jax.ShapeDtypeStruct((m, D_MODEL), F32),
            compiler_params=_cparams(1))(x, m_in, gpost_all), None
    return pl.pallas_call(
        _resid_norm_kernel, grid=(m // tm,),
        in_specs=[row, row, pl.BlockSpec((None, 1, D_MODEL), lambda i: (l, 0, 0)),
                  pl.BlockSpec((None, 1, D_MODEL), lambda i: (l_next, 0, 0))],
        out_specs=[row, row],
        out_shape=[jax.ShapeDtypeStruct((m, D_MODEL), F32), jax.ShapeDtypeStruct((m, D_MODEL), BF16)],
        compiler_params=_cparams(1))(x, m_in, gpost_all, gnext_all)


def _mm_kernel(a_ref, w_ref, o_ref, wb_ref):
    @pl.when(pl.program_id(1) == 0)
    def _():
        wb_ref[...] = w_ref[...].astype(BF16)

    o_ref[...] = jnp.dot(a_ref[...], wb_ref[...], preferred_element_type=F32).astype(o_ref.dtype)


def _matmul(a, w_all, l, tm, tn, out_dtype=F32, col0=0, n=None):
    m, k = a.shape
    n = w_all.shape[2] if n is None else n
    j0 = col0 // tn
    return pl.pallas_call(
        _mm_kernel, grid=(n // tn, m // tm),
        in_specs=[pl.BlockSpec((tm, k), lambda j, i: (i, 0)),
                  pl.BlockSpec((None, k, tn), lambda j, i: (l, 0, j0 + j))],
        out_specs=pl.BlockSpec((tm, tn), lambda j, i: (i, j)),
        out_shape=jax.ShapeDtypeStruct((m, n), out_dtype),
        scratch_shapes=[pltpu.VMEM((k, tn), BF16)],
        compiler_params=_cparams(2))(a, w_all)


def _merge_kernel(ya_ref, yb_ref, yc_ref, ga_ref, gb_ref, gc_ref, wa_ref, wb_ref, wc_ref, o_ref,
                  wab, wbb, wcb):
    @pl.when(pl.program_id(1) == 0)
    def _():
        wab[...] = wa_ref[...].astype(BF16)
        wbb[...] = wb_ref[...].astype(BF16)
        wcb[...] = wc_ref[...].astype(BF16)

    acc = _sigmoid(ga_ref[...]) * jnp.dot(ya_ref[...], wab[...], preferred_element_type=F32)
    acc = acc + _sigmoid(gb_ref[...]) * jnp.dot(yb_ref[...], wbb[...], preferred_element_type=F32)
    acc = acc + _sigmoid(gc_ref[...]) * jnp.dot(yc_ref[...], wcb[...], preferred_element_type=F32)
    o_ref[...] = acc.astype(o_ref.dtype)


def _merge(ya, yb, yc, proj, wa_all, wb_all, wc_all, l, tm):
    m = ya.shape[0]
    tn = CB
    return pl.pallas_call(
        _merge_kernel, grid=(D_MODEL // tn, m // tm),
        in_specs=[pl.BlockSpec((tm, W_A), lambda j, i: (i, 0)),
                  pl.BlockSpec((tm, W_B), lambda j, i: (i, 0)),
                  pl.BlockSpec((tm, W_C), lambda j, i: (i, 0)),
                  pl.BlockSpec((tm, tn), lambda j, i: (i, COL_GATE_A + j)),
                  pl.BlockSpec((tm, tn), lambda j, i: (i, COL_GATE_B + j)),
                  pl.BlockSpec((tm, tn), lambda j, i: (i, COL_GATE_C + j)),
                  pl.BlockSpec((None, W_A, tn), lambda j, i: (l, 0, j)),
                  pl.BlockSpec((None, W_B, tn), lambda j, i: (l, 0, j)),
                  pl.BlockSpec((None, W_C, tn), lambda j, i: (l, 0, j))],
        out_specs=pl.BlockSpec((tm, tn), lambda j, i: (i, j)),
        out_shape=jax.ShapeDtypeStruct((m, D_MODEL), BF16),
        scratch_shapes=[pltpu.VMEM((W_A, tn), BF16), pltpu.VMEM((W_B, tn), BF16),
                        pltpu.VMEM((W_C, tn), BF16)],
        compiler_params=_cparams(2))(ya, yb, yc, proj, proj, proj, wa_all, wb_all, wc_all)


def _ffn_up_kernel(*refs, tm, shift, halo, tiles_per_seq, has_init):
    if has_init:
        (a_ref, wa_ref, wg_ref, cwa_ref, cwg_ref, cba_ref, cbg_ref, ia_ref, ig_ref,
         act_ref, sta_ref, stg_ref, wab, wgb, ua, ug) = refs
    else:
        (a_ref, wa_ref, wg_ref, cwa_ref, cwg_ref, cba_ref, cbg_ref,
         act_ref, sta_ref, stg_ref, wab, wgb, ua, ug) = refs
    i = pl.program_id(1)
    c = i % tiles_per_seq

    @pl.when(i == 0)
    def _():
        wab[...] = wa_ref[...].astype(BF16)
        wgb[...] = wg_ref[...].astype(BF16)

    if has_init:
        ua[0:halo, :] = ia_ref[...]
        ug[0:halo, :] = ig_ref[...]
    else:
        @pl.when(c == 0)
        def _():
            ua[0:halo, :] = jnp.zeros((halo, ua.shape[1]), F32)
            ug[0:halo, :] = jnp.zeros((halo, ug.shape[1]), F32)

        @pl.when(c > 0)
        def _():
            ua[0:halo, :] = ua[tm:tm + halo, :]
            ug[0:halo, :] = ug[tm:tm + halo, :]

    a = a_ref[...]
    ua[halo:halo + tm, :] = jnp.dot(a, wab[...], preferred_element_type=F32)
    ug[halo:halo + tm, :] = jnp.dot(a, wgb[...], preferred_element_type=F32)

    def conv(u, cw_ref, cb_ref):
        cw = cw_ref[...]
        y = u[halo - 2 * shift:halo - 2 * shift + tm, :] * cw[0:1, :]
        y = y + u[halo - shift:halo - shift + tm, :] * cw[1:2, :]
        y = y + u[halo:halo + tm, :] * cw[2:3, :]
        return y + cb_ref[...]

    act = _gelu_tanh(conv(ug, cwg_ref, cbg_ref)) * conv(ua, cwa_ref, cba_ref)
    act_ref[...] = act.astype(act_ref.dtype)

    ns = (CONV_F - 1) * shift

    @pl.when(c == tiles_per_seq - 1)
    def _():
        sta_ref[...] = ua[halo + tm - ns:halo + tm, :]
        stg_ref[...] = ug[halo + tm - ns:halo + tm, :]


def _ffn_up(a, w_up_all, cw_all, cb_all, l, *, tm, shift, halo, tiles_per_seq, init=None):
    m = a.shape[0]
    tn = CB
    nj = D_FF // tn
    n_seq_tiles = m // tm
    has_init = init is not None
    ns = (CONV_F - 1) * shift
    in_specs = [pl.BlockSpec((tm, D_MODEL), lambda j, i: (i, 0)),
                pl.BlockSpec((None, D_MODEL, tn), lambda j, i: (l, 0, j)),
                pl.BlockSpec((None, D_MODEL, tn), lambda j, i: (l, 0, nj + j)),
                pl.BlockSpec((None, CONV_F, tn), lambda j, i: (l, 0, j)),
                pl.BlockSpec((None, CONV_F, tn), lambda j, i: (l, 0, nj + j)),
                pl.BlockSpec((None, 1, tn), lambda j, i: (l, 0, j)),
                pl.BlockSpec((None, 1, tn), lambda j, i: (l, 0, nj + j))]
    args = [a, w_up_all, w_up_all, cw_all, cw_all, cb_all, cb_all]
    if has_init:
        in_specs += [pl.BlockSpec((halo, tn), lambda j, i: (0, j)),
                     pl.BlockSpec((halo, tn), lambda j, i: (0, nj + j))]
        args += [init, init]
        st_shape = jax.ShapeDtypeStruct((ns, D_FF), F32)
        st_spec = pl.BlockSpec((ns, tn), lambda j, i: (0, j))
    else:
        n_seq = n_seq_tiles // tiles_per_seq
        st_shape = jax.ShapeDtypeStruct((n_seq, ns, D_FF), F32)
        st_spec = pl.BlockSpec((None, ns, tn), lambda j, i: (i // tiles_per_seq, 0, j))
    kern = functools.partial(_ffn_up_kernel, tm=tm, shift=shift, halo=halo,
                             tiles_per_seq=tiles_per_seq, has_init=has_init)
    return pl.pallas_call(
        kern, grid=(nj, n_seq_tiles),
        in_specs=in_specs,
        out_specs=[pl.BlockSpec((tm, tn), lambda j, i: (i, j)), st_spec, st_spec],
        out_shape=[jax.ShapeDtypeStruct((m, D_FF), BF16), st_shape, st_shape],
        scratch_shapes=[pltpu.VMEM((D_MODEL, tn), BF16), pltpu.VMEM((D_MODEL, tn), BF16),
                        pltpu.VMEM((halo + tm, tn), F32), pltpu.VMEM((halo + tm, tn), F32)],
        compiler_params=_cparams(2))(*args)


def _lru_coeffs(xbc, wri_ref, br_ref, bi_ref, lam_ref):
    ri = jnp.dot(xbc.astype(BF16), wri_ref[...], preferred_element_type=F32)
    r = _sigmoid(ri[:, :W_B] + br_ref[...])
    ig = _sigmoid(ri[:, W_B:] + bi_ref[...])
    log_a = (-LRU_C) * r * _softplus(-lam_ref[...])
    a = jnp.exp(log_a)
    mult = jnp.sqrt(1.0 - jnp.exp(2.0 * log_a))
    return a, mult, ig * xbc


def _bc_prompt_kernel(xb_ref, xc_ref, gb_ref, gc_ref, cw_ref, cb_ref, wri_ref, br_ref, bi_ref,
                      lam_ref, sw_ref, yb_ref, yc_ref, lbuf_ref, hlast_ref, sbuf_ref,
                      xs, us, a_s, b_s, h_s, hprev, *, tm, tiles_per_seq):
    c = pl.program_id(1)
    first = c == 0

    @pl.when(first)
    def _():
        xs[0:8, :] = jnp.zeros((8, W_B), F32)
        us[0:8, :] = jnp.zeros((8, W_C), F32)
        hprev[...] = jnp.zeros((8, W_B), F32)

    @pl.when(c > 0)
    def _():
        xs[0:8, :] = xs[tm:tm + 8, :]
        us[0:8, :] = us[tm:tm + 8, :]

    xs[8:8 + tm, :] = xb_ref[...]
    us[8:8 + tm, :] = gc_ref[...] * xc_ref[...]

    cw = cw_ref[...]
    xbc = cb_ref[...] + xs[8:8 + tm, :] * cw[3:4, :]
    for i in range(CONV_B - 1):
        xbc = xbc + xs[5 + i:5 + i + tm, :] * cw[i:i + 1, :]
    a, mult, ix = _lru_coeffs(xbc, wri_ref, br_ref, bi_ref, lam_ref)
    a_s[...] = a
    b_s[...] = mult * ix

    @pl.when(first)
    def _():
        b_s[0:1, :] = ix[0:1, :]

    row8 = lax.broadcasted_iota(jnp.int32, (8, W_B), 0)

    def tile_body(t, hp):
        off = pl.multiple_of(t * 8, 8)
        av = a_s[pl.ds(off, 8), :]
        bv = b_s[pl.ds(off, 8), :]
        for k in (1, 2, 4):
            a_sh = jnp.where(row8 >= k, pltpu.roll(av, k, 0), 1.0)
            b_sh = jnp.where(row8 >= k, pltpu.roll(bv, k, 0), 0.0)
            bv = av * b_sh + bv
            av = av * a_sh
        hv = av * hp + bv
        h_s[pl.ds(off, 8), :] = hv
        return jnp.broadcast_to(hv[7:8, :], (8, W_B))

    hprev[...] = lax.fori_loop(0, tm // 8, tile_body, hprev[...])
    yb_ref[...] = h_s[...].astype(yb_ref.dtype)

    sw = sw_ref[...]
    yc = us[6:6 + tm, :] * sw[0:1, :] + us[7:7 + tm, :] * sw[1:2, :] + us[8:8 + tm, :] * sw[2:3, :]
    yc_ref[...] = (gb_ref[...] * yc).astype(yc_ref.dtype)

    @pl.when(c == tiles_per_seq - 1)
    def _():
        lbuf_ref[...] = xs[8 + tm - (CONV_B - 1):8 + tm, :]
        hlast_ref[...] = h_s[tm - 1:tm, :]
        sbuf_ref[...] = us[8 + tm - (CONV_C - 1):8 + tm, :]


def _bc_prompt(proj, n_seq, tm, tiles_per_seq, cw_all, cb_all, wri_all, br_all, bi_all, lam_all,
               sw_all, l):
    m = proj.shape[0]
    col = lambda cidx: pl.BlockSpec((tm, CB), lambda b, c: (b * tiles_per_seq + c, cidx))
    par = lambda rows, width: pl.BlockSpec((None, rows, width), lambda b, c: (l, 0, 0))
    seq_out = lambda rows: pl.BlockSpec((None, rows, CB), lambda b, c: (b, 0, 0))
    row_out = pl.BlockSpec((tm, CB), lambda b, c: (b * tiles_per_seq + c, 0))
    kern = functools.partial(_bc_prompt_kernel, tm=tm, tiles_per_seq=tiles_per_seq)
    return pl.pallas_call(
        kern, grid=(n_seq, tiles_per_seq),
        in_specs=[col(COL_XB), col(COL_XC), col(COL_GB), col(COL_GC),
                  par(CONV_B, W_B), par(1, W_B), par(W_B, 2 * W_B), par(1, W_B), par(1, W_B),
                  par(1, W_B), par(CONV_C, W_C)],
        out_specs=[row_out, row_out, seq_out(CONV_B - 1), seq_out(1), seq_out(CONV_C - 1)],
        out_shape=[jax.ShapeDtypeStruct((m, W_B), BF16), jax.ShapeDtypeStruct((m, W_C), BF16),
                   jax.ShapeDtypeStruct((n_seq, CONV_B - 1, W_B), F32),
                   jax.ShapeDtypeStruct((n_seq, 1, W_B), F32),
                   jax.ShapeDtypeStruct((n_seq, CONV_C - 1, W_C), F32)],
        scratch_shapes=[pltpu.VMEM((8 + tm, W_B), F32), pltpu.VMEM((8 + tm, W_C), F32),
                        pltpu.VMEM((tm, W_B), F32), pltpu.VMEM((tm, W_B), F32),
                        pltpu.VMEM((tm, W_B), F32), pltpu.VMEM((8, W_B), F32)],
        compiler_params=_cparams(2))(proj, proj, proj, proj, cw_all, cb_all, wri_all, br_all,
                                     bi_all, lam_all, sw_all)


def _bc_sample_kernel(xb_ref, xc_ref, gb_ref, gc_ref, lbuf_in, h0_ref, sbuf_in, cw_ref, cb_ref,
                      wri_ref, br_ref, bi_ref, lam_ref, sw_ref, yb_ref, yc_ref, lbuf_ref, hlast_ref,
                      sbuf_ref, xs, us, *, nb, nt):
    m = nb * nt
    hb = (CONV_B - 1) * nb
    hc = (CONV_C - 1) * nb
    xs[0:hb, :] = lbuf_in[...]
    xs[hb:hb + m, :] = xb_ref[...]
    us[0:hc, :] = sbuf_in[...]
    us[hc:hc + m, :] = gc_ref[...] * xc_ref[...]

    cw = cw_ref[...]
    xbc = cb_ref[...] + xs[hb:hb + m, :] * cw[3:4, :]
    for i in range(CONV_B - 1):
        xbc = xbc + xs[i * nb:i * nb + m, :] * cw[i:i + 1, :]
    a, mult, ix = _lru_coeffs(xbc, wri_ref, br_ref, bi_ref, lam_ref)
    bterm = mult * ix
    h = h0_ref[...]
    for t in range(nt):
        h = a[t * nb:(t + 1) * nb, :] * h + bterm[t * nb:(t + 1) * nb, :]
        yb_ref[t * nb:(t + 1) * nb, :] = h.astype(yb_ref.dtype)
    hlast_ref[...] = h
    lbuf_ref[...] = xs[m:m + hb, :]

    sw = sw_ref[...]
    yc = us[0:m, :] * sw[0:1, :] + us[nb:nb + m, :] * sw[1:2, :] + us[2 * nb:2 * nb + m, :] * sw[2:3, :]
    yc_ref[...] = (gb_ref[...] * yc).astype(yc_ref.dtype)
    sbuf_ref[...] = us[m:m + hc, :]


def _bc_sample(proj, nb, nt, lbuf, h0, sbuf, cw_all, cb_all, wri_all, br_all, bi_all, lam_all,
               sw_all, l):
    m = nb * nt
    hb = (CONV_B - 1) * nb
    hc = (CONV_C - 1) * nb
    col = lambda cidx: pl.BlockSpec((m, CB), lambda i: (0, cidx))
    full = lambda rows: pl.BlockSpec((rows, CB), lambda i: (0, 0))
    par = lambda rows, width: pl.BlockSpec((None, rows, width), lambda i: (l, 0, 0))
    kern = functools.partial(_bc_sample_kernel, nb=nb, nt=nt)
    return pl.pallas_call(
        kern, grid=(1,),
        in_specs=[col(COL_XB), col(COL_XC), col(COL_GB), col(COL_GC), full(hb), full(nb), full(hc),
                  par(CONV_B, W_B), par(1, W_B), par(W_B, 2 * W_B), par(1, W_B), par(1, W_B),
                  par(1, W_B), par(CONV_C, W_C)],
        out_specs=[full(m), full(m), full(hb), full(nb), full(hc)],
        out_shape=[jax.ShapeDtypeStruct((m, W_B), BF16), jax.ShapeDtypeStruct((m, W_C), BF16),
                   jax.ShapeDtypeStruct((hb, W_B), F32), jax.ShapeDtypeStruct((nb, W_B), F32),
                   jax.ShapeDtypeStruct((hc, W_C), F32)],
        scratch_shapes=[pltpu.VMEM((hb + m, W_B), F32), pltpu.VMEM((hc + m, W_C), F32)],
        compiler_params=_cparams(1))(proj, proj, proj, proj, lbuf, h0, sbuf, cw_all, cb_all,
                                     wri_all, br_all, bi_all, lam_all, sw_all)


def _split_bf16(x):
    hi = x.astype(BF16)
    lo = (x - hi.astype(F32)).astype(BF16)
    return hi, lo


def _sb_block(qt, kblk, vblk, tri2, bias, carry, acc, mask):
    z = lax.dot_general(qt, kblk, (((1,), (1,)), ((), ())), preferred_element_type=F32) + bias
    sp = _softplus(z)
    lk = -sp if mask is None else jnp.where(mask, -sp, 0.0)
    hi, lo = _split_bf16(lk)
    btw = (jnp.dot(jnp.concatenate([hi, lo], axis=1), tri2, preferred_element_type=F32)
           + jnp.tile(carry, (1, lk.shape[1] // DH)))
    w = jnp.exp(z - sp + btw)
    if mask is not None:
        w = jnp.where(mask, w, 0.0)
    acc = acc + jnp.dot(w.astype(BF16), vblk, preferred_element_type=F32)
    carry = carry + jnp.broadcast_to(jnp.sum(lk, axis=1, keepdims=True), carry.shape)
    return acc, carry


def _attn_prompt_kernel(bias_ref, q_ref, k_ref, v_ref, tri_k_ref, tri_m_ref, o_ref, qb, kb, vb, *,
                        layer, tq, tk, n_real, unroll):
    bias = bias_ref[layer, pl.program_id(1)]
    qb[...] = (q_ref[...] * SCALE).astype(BF16)
    kb[...] = k_ref[...].astype(BF16)
    vb[...] = v_ref[...].astype(BF16)
    tri_k = tri_k_ref[...]
    tri_m = tri_m_ref[...]
    meta_mask = lax.broadcasted_iota(jnp.int32, (tq, PAGE), 1) < N_META
    kpq = tq // tk

    for qi in range(n_real // tq):
        q0 = N_META + tq * qi
        qt = qb[q0:q0 + tq, :]
        acc = jnp.zeros((tq, DH), F32)
        carry = jnp.zeros((tq, DH), F32)
        for d in reversed(range(kpq)):
            k0 = q0 + d * tk
            r0 = d * tk
            vis = (lax.broadcasted_iota(jnp.int32, (tq - r0, tk), 1)
                   < lax.broadcasted_iota(jnp.int32, (tq - r0, tk), 0))
            a_new, c_new = _sb_block(qt[r0:, :], kb[k0:k0 + tk, :], vb[k0:k0 + tk, :], tri_k, bias,
                                     carry[r0:, :], acc[r0:, :], vis)
            acc = a_new if r0 == 0 else jnp.concatenate([acc[:r0, :], a_new], axis=0)
            carry = c_new if r0 == 0 else jnp.concatenate([carry[:r0, :], c_new], axis=0)
        n_low = qi * kpq
        if n_low > 0:
            def body(jj, st, n_low=n_low, qt=qt):
                off = pl.multiple_of(N_META + (n_low - 1 - jj) * tk, 16)
                return _sb_block(qt, kb[pl.ds(off, tk), :], vb[pl.ds(off, tk), :], tri_k, bias,
                                 st[1], st[0], None)

            acc, carry = lax.fori_loop(0, n_low, body, (acc, carry), unroll=unroll)
        acc, carry = _sb_block(qt, kb[0:PAGE, :], vb[0:PAGE, :], tri_m, bias, carry, acc, meta_mask)
        o_ref[q0:q0 + tq, :] = acc.astype(o_ref.dtype)

    mm = (lax.broadcasted_iota(jnp.int32, (N_META, PAGE), 1)
          < lax.broadcasted_iota(jnp.int32, (N_META, PAGE), 0))
    acc, _ = _sb_block(qb[0:N_META, :], kb[0:PAGE, :], vb[0:PAGE, :], tri_m, bias,
                       jnp.zeros((N_META, DH), F32), jnp.zeros((N_META, DH), F32), mm)
    o_ref[0:N_META, :] = acc.astype(o_ref.dtype)


def _later_key_matrix(n):
    return (lax.broadcasted_iota(jnp.int32, (n, n), 0)
            > lax.broadcasted_iota(jnp.int32, (n, n), 1)).astype(BF16)


def _attn_prompt(q, k, v, sb_bias, l, n_seq, t_seq, tq, tk, unroll):
    m = q.shape[0]
    kern = functools.partial(_attn_prompt_kernel, layer=l, tq=tq, tk=tk, n_real=t_seq - N_META,
                             unroll=unroll)
    head = pl.BlockSpec((t_seq, DH), lambda b, h: (b, h))
    const = lambda n: pl.BlockSpec((2 * n, n), lambda b, h: (0, 0))
    tri2 = lambda n: jnp.concatenate([_later_key_matrix(n)] * 2, axis=0)
    return pl.pallas_call(
        kern, grid=(n_seq, H_A),
        in_specs=[pl.BlockSpec(memory_space=pltpu.SMEM), head, head, head, const(tk), const(PAGE)],
        out_specs=head,
        out_shape=jax.ShapeDtypeStruct((m, W_A), BF16),
        scratch_shapes=[pltpu.VMEM((t_seq, DH), BF16)] * 3,
        compiler_params=_cparams(2))(sb_bias, q, k, v, tri2(tk), tri2(PAGE))


def _attn_sample_kernel(*refs, nt, pps):
    n_pg = 2 * pps
    q_ref, kn_ref, vn_ref = refs[1:4]
    k_refs = refs[4:4 + n_pg]
    v_refs = refs[4 + n_pg:4 + 2 * n_pg]
    bias_ref, tri_ref, o_ref, qf, qbd, kpad, vpad, k2s, v2s, acc, carry = refs[4 + 2 * n_pg:]
    j = pl.program_id(1)
    lanes = 2 * H_A * nt
    wk = 2 * W_A

    def step(k2, v2, tri, mask):
        z = jnp.dot(k2, qbd[...], preferred_element_type=F32) * SCALE + bias_ref[...]
        sp = _softplus(z)
        lk = -sp if mask is None else jnp.where(mask, -sp, 0.0)
        hi, lo = _split_bf16(lk)
        bt = jnp.dot(tri, jnp.concatenate([hi, lo], axis=1), preferred_element_type=F32)
        w = jnp.exp(z - sp + bt[:, :lanes] + bt[:, lanes:] + carry[...])
        if mask is not None:
            w = jnp.where(mask, w, 0.0)
        acc[...] += jnp.dot(w.T.astype(BF16), v2, preferred_element_type=F32)
        carry[...] += jnp.sum(lk, axis=0, keepdims=True)

    @pl.when(j == 0)
    def _():
        qf[...] = jnp.zeros((lanes, wk), F32)
        kpad[...] = jnp.zeros((PAGE, wk), F32)
        vpad[...] = jnp.zeros((PAGE, wk), F32)
        for b in range(2):
            kpad[0:nt, b * W_A:(b + 1) * W_A] = kn_ref[b]
            vpad[0:nt, b * W_A:(b + 1) * W_A] = vn_ref[b]
            for h in range(H_A):
                r0 = (b * H_A + h) * nt
                c0 = b * W_A + h * DH
                qf[r0:r0 + nt, c0:c0 + DH] = q_ref[b, :, h * DH:(h + 1) * DH]
        qbd[...] = qf[...].T.astype(BF16)
        acc[...] = jnp.zeros((lanes, wk), F32)
        carry[...] = jnp.zeros((1, lanes), F32)
        row = lax.broadcasted_iota(jnp.int32, (PAGE, lanes), 0)
        tq = lax.broadcasted_iota(jnp.int32, (PAGE, lanes), 1) % nt
        step(kpad[...].astype(BF16), vpad[...].astype(BF16), tri_ref[0:PAGE, 0:PAGE], row < tq)

    for b in range(2):
        for i in range(pps):
            for h in range(H_A):
                c0 = b * W_A + h * DH
                k2s[i * PAGE:(i + 1) * PAGE, c0:c0 + DH] = (
                    k_refs[b * pps + i][pl.ds(h, PAGE, stride=H_A), :].astype(BF16))
                v2s[i * PAGE:(i + 1) * PAGE, c0:c0 + DH] = (
                    v_refs[b * pps + i][pl.ds(h, PAGE, stride=H_A), :].astype(BF16))
    step(k2s[...], v2s[...], tri_ref[...], None)

    @pl.when(j == pl.num_programs(1) - 1)
    def _():
        for b in range(2):
            for h in range(H_A):
                r0 = (b * H_A + h) * nt
                c0 = b * W_A + h * DH
                o_ref[b, :, h * DH:(h + 1) * DH] = acc[r0:r0 + nt, c0:c0 + DH]


def _attn_sample(q, kn, vn, cache_k, cache_v, page_table, bias_lanes, l, pps=4):
    nb, nt, _ = q.shape
    n_pages = page_table.shape[1]
    pt = page_table.reshape(-1)
    lanes = 2 * H_A * nt
    wk = 2 * W_A
    rows = pps * PAGE
    tok = pl.BlockSpec((2, nt, W_A), lambda p, j, pt_ref: (p, 0, 0))

    def page(which, i):
        def index_map(p, j, pt_ref):
            return (l, pt_ref[(2 * p + which) * n_pages + n_pages - (j + 1) * pps + i], 0, 0)
        return pl.BlockSpec((None, None, PAGE * H_A, DH), index_map)

    pages = [page(b, i) for b in range(2) for i in range(pps)]
    kern = functools.partial(_attn_sample_kernel, nt=nt, pps=pps)
    grid_spec = pltpu.PrefetchScalarGridSpec(
        num_scalar_prefetch=1, grid=(nb // 2, n_pages // pps),
        in_specs=[tok, tok, tok] + pages + pages + [
            pl.BlockSpec((1, lanes), lambda p, j, pt_ref: (0, 0)),
            pl.BlockSpec((rows, rows), lambda p, j, pt_ref: (0, 0))],
        out_specs=tok,
        scratch_shapes=[pltpu.VMEM((lanes, wk), F32), pltpu.VMEM((wk, lanes), BF16),
                        pltpu.VMEM((PAGE, wk), F32), pltpu.VMEM((PAGE, wk), F32),
                        pltpu.VMEM((rows, wk), BF16), pltpu.VMEM((rows, wk), BF16),
                        pltpu.VMEM((lanes, wk), F32), pltpu.VMEM((1, lanes), F32)])
    tri = _later_key_matrix(rows).T
    return pl.pallas_call(
        kern, grid_spec=grid_spec, out_shape=jax.ShapeDtypeStruct((nb, nt, W_A), F32),
        compiler_params=_cparams(2))(pt, q, kn, vn, *([cache_k] * (2 * pps)),
                                     *([cache_v] * (2 * pps)), bias_lanes, tri)


def _block_diag(w):
    depth, nblk, c, d = w.shape
    eye = jnp.eye(nblk, dtype=w.dtype)
    return jnp.einsum('lncd,nm->lncmd', w, eye).reshape(depth, nblk * c, nblk * d)


def kernel(x_prompt, x_sample, cache_k, cache_v, state_lru_conv, state_lru_h, state_sconv, state_ffn_conv, page_table, meta_tokens, norm_mix_pre, norm_mix_post, norm_ffn_pre, norm_ffn_post, w_in, sb_bias, lru_conv_w, lru_conv_b, lru_w_r, lru_b_r, lru_w_i, lru_b_i, lru_lambda, sconv_w, w_out_a, w_out_b, w_out_c, w_o, ffn_w_up, ffn_conv_w, ffn_conv_b, ffn_w_down):
    bp, seq, _ = x_prompt.shape
    nb, nt, _ = x_sample.shape
    t_seq = N_META + seq
    n_pool = cache_k.shape[1]

    row3 = lambda p: p.reshape(DEPTH, 1, p.shape[-1])
    g_mix_pre, g_mix_post = row3(norm_mix_pre), row3(norm_mix_post)
    g_ffn_pre, g_ffn_post = row3(norm_ffn_pre), row3(norm_ffn_post)
    lcb, lbr, lbi, lam = row3(lru_conv_b), row3(lru_b_r), row3(lru_b_i), row3(lru_lambda)
    fcb = row3(ffn_conv_b)
    wri = jnp.concatenate([_block_diag(lru_w_r), _block_diag(lru_w_i)], axis=-1).astype(BF16)
    ck = cache_k.reshape(DEPTH, n_pool, PAGE * H_A, DH)
    cv = cache_v.reshape(DEPTH, n_pool, PAGE * H_A, DH)

    tp = t_seq // 3
    ms = nb * nt

    def layer(x, xn, l, prompt):
        last = l == DEPTH - 1
        if prompt:
            tm_big, tm, tm_row = t_seq, tp, tp
        else:
            tm_big, tm, tm_row = ms, ms, ms // 4
        q = _matmul(xn, w_in, l, tm_big, CB, col0=0, n=W_A)
        k = _matmul(xn, w_in, l, tm_big, CB, col0=W_A, n=W_A)
        v = _matmul(xn, w_in, l, tm_big, CB, col0=2 * W_A, n=W_A)
        proj = _matmul(xn, w_in, l, tm_big, CB, col0=3 * W_A, n=N_IN - 3 * W_A)
        if prompt:
            ya = _attn_prompt(q, k, v, sb_bias, l, bp, t_seq, 1024, 256, 2)
            yb, yc, lbuf, hlast, sbuf = _bc_prompt(proj, bp, tm, 3, lru_conv_w, lcb, wri, lbr, lbi,
                                                   lam, sconv_w, l)
            hlast = hlast.reshape(bp, W_B)
            k_out = k.reshape(bp, t_seq, H_A, DH)
            v_out = v.reshape(bp, t_seq, H_A, DH)
        else:
            to_bm = lambda a: a.reshape(-1, nb, a.shape[-1]).transpose(1, 0, 2)
            to_tm = lambda a: a.transpose(1, 0, 2).reshape(a.shape[1] * nb, a.shape[-1])
            q_bm, k_bm, v_bm = to_bm(q), to_bm(k), to_bm(v)
            bias_lanes = jnp.tile(jnp.repeat(sb_bias[l], nt), 2).reshape(1, 2 * H_A * nt)
            ya = to_tm(_attn_sample(q_bm, k_bm, v_bm, ck, cv, page_table, bias_lanes, l)).astype(BF16)
            yb, yc, lbuf, hlast, sbuf = _bc_sample(
                proj, nb, nt, to_tm(state_lru_conv[l]), state_lru_h[l], to_tm(state_sconv[l]),
                lru_conv_w, lcb, wri, lbr, lbi, lam, sconv_w, l)
            lbuf, sbuf = to_bm(lbuf), to_bm(sbuf)
            k_out = k_bm.reshape(nb, nt, H_A, DH)
            v_out = v_bm.reshape(nb, nt, H_A, DH)
        merged = _merge(ya, yb, yc, proj, w_out_a, w_out_b, w_out_c, l, tm)
        mo = _matmul(merged, w_o, l, tm_big, CB)
        x, xn = _resid_norm(x, mo, g_mix_post, l, g_ffn_pre, l, tm_row)
        if prompt:
            act, sta, stg = _ffn_up(xn, ffn_w_up, ffn_conv_w, fcb, l, tm=tm, shift=1, halo=8,
                                    tiles_per_seq=3)
            fbuf = jnp.concatenate([sta, stg], axis=-1)
        else:
            act, sta, stg = _ffn_up(xn, ffn_w_up, ffn_conv_w, fcb, l, tm=tm, shift=nb,
                                    halo=(CONV_F - 1) * nb, tiles_per_seq=1,
                                    init=to_tm(state_ffn_conv[l]))
            fbuf = to_bm(jnp.concatenate([sta, stg], axis=-1))
        f = _matmul(act, ffn_w_down, l, tm, CB if prompt else CB // 2)
        x, xn = _resid_norm(x, f, g_ffn_post, l, None if last else g_mix_pre, l + 1, tm_row)
        return x, xn, (k_out, v_out, lbuf, hlast, sbuf, fbuf)

    def run(x, prompt, tm):
        xn = _rmsnorm_cast(x, g_mix_pre, 0, tm)
        outs = []
        for l in range(DEPTH):
            x, xn, o = layer(x, xn, l, prompt)
            outs.append(o)
        return x, [jnp.stack(z) for z in zip(*outs)]

    dt = x_prompt.dtype
    xp = jnp.concatenate([jnp.broadcast_to(meta_tokens.astype(dt)[None], (bp, N_META, D_MODEL)),
                          x_prompt], axis=1).reshape(bp * t_seq, D_MODEL)
    xp, (kp, vp, lbp, hlp, sbp, fbp) = run(xp, True, tp)
    y_prompt = xp.reshape(bp, t_seq, D_MODEL)[:, N_META:]

    xs = x_sample.transpose(1, 0, 2).reshape(ms, D_MODEL)
    xs, (ks, vs, lbs, hls, sbs, fbs) = run(xs, False, ms // 4)
    y_sample = xs.reshape(nt, nb, D_MODEL).transpose(1, 0, 2)

    return (y_prompt, y_sample, kp, vp, ks, vs, lbp, lbs, hlp, hls, sbp, sbs, fbp, fbs)
```

```python
import functools

import jax
import jax.numpy as jnp
from jax import lax
from jax.experimental import pallas as pl
from jax.experimental.pallas import tpu as pltpu

F32 = jnp.float32
BF16 = jnp.bfloat16

D_MODEL = 2048
DEPTH = 2
N_META = 16
H_A = 8
DH = 128
W_A = H_A * DH
W_B = 512
NB_LRU = 8
W_C = 512
CONV_B = 4
CONV_C = 3
CONV_F = 3
D_FF = 5632
N_BRANCH = 3
LRU_C = 8.0
EPS = 1e-6
SCALE = DH ** -0.5
N_IN = 3 * W_A + W_B + 3 * W_C + N_BRANCH * D_MODEL
PAGE = 128

CB = 512
COL_XB, COL_XC, COL_GB, COL_GC = 0, 1, 2, 3
COL_GATE_A, COL_GATE_B, COL_GATE_C = 4, 8, 12

VMEM_LIMIT_BYTES = 58 * 1024 * 1024


def _cparams(n_axes):
    return pltpu.CompilerParams(dimension_semantics=("arbitrary",) * n_axes,
                                vmem_limit_bytes=VMEM_LIMIT_BYTES)


def _sigmoid(x):
    return 1.0 / (1.0 + jnp.exp(-x))


def _softplus(x):
    return jnp.maximum(x, 0.0) + jnp.log(1.0 + jnp.exp(-jnp.abs(x)))


def _gelu_tanh(g):
    return 0.5 * g * (1.0 + jnp.tanh(0.7978845608028654 * (g + 0.044715 * (g * g * g))))


def _rms(x, g):
    return x * lax.rsqrt(jnp.mean(x * x, axis=-1, keepdims=True) + EPS) * g


def _rmsnorm_kernel(x_ref, g_ref, o_ref):
    o_ref[...] = _rms(x_ref[...], g_ref[...]).astype(o_ref.dtype)


def _rmsnorm_cast(x, g_all, l, tm):
    m = x.shape[0]
    return pl.pallas_call(
        _rmsnorm_kernel, grid=(m // tm,),
        in_specs=[pl.BlockSpec((tm, D_MODEL), lambda i: (i, 0)),
                  pl.BlockSpec((None, 1, D_MODEL), lambda i: (l, 0, 0))],
        out_specs=pl.BlockSpec((tm, D_MODEL), lambda i: (i, 0)),
        out_shape=jax.ShapeDtypeStruct((m, D_MODEL), BF16),
        compiler_params=_cparams(1))(x, g_all)


def _resid_norm_kernel(x_ref, m_ref, gpost_ref, gnext_ref, xo_ref, xn_ref):
    x1 = x_ref[...] + _rms(m_ref[...], gpost_ref[...])
    xo_ref[...] = x1
    xn_ref[...] = _rms(x1, gnext_ref[...]).astype(xn_ref.dtype)


def _resid_kernel(x_ref, m_ref, gpost_ref, xo_ref):
    xo_ref[...] = x_ref[...] + _rms(m_ref[...], gpost_ref[...])


def _resid_norm(x, m_in, gpost_all, l, gnext_all, l_next, tm):
    m = x.shape[0]
    row = pl.BlockSpec((tm, D_MODEL), lambda i: (i, 0))
    if gnext_all is None:
        return pl.pallas_call(
            _resid_kernel, grid=(m // tm,),
            in_specs=[row, row, pl.BlockSpec((None, 1, D_MODEL), lambda i: (l, 0, 0))],
            out_specs=row, out_shape=jax.ShapeDtypeStruct((m, D_MODEL), F32),
            compiler_params=_cparams(1))(x, m_in, gpost_all), None
    return pl.pallas_call(
        _resid_norm_kernel, grid=(m // tm,),
        in_specs=[row, row, pl.BlockSpec((None, 1, D_MODEL), lambda i: (l, 0, 0)),
                  pl.BlockSpec((None, 1, D_MODEL), lambda i: (l_next, 0, 0))],
        out_specs=[row, row],
        out_shape=[jax.ShapeDtypeStruct((m, D_MODEL), F32), jax.ShapeDtypeStruct((m, D_MODEL), BF16)],
        compiler_params=_cparams(1))(x, m_in, gpost_all, gnext_all)


def _mm_kernel(a_ref, w_ref, o_ref, wb_ref):
    @pl.when(pl.program_id(1) == 0)
    def _():
        wb_ref[...] = w_ref[...].astype(BF16)

    o_ref[...] = jnp.dot(a_ref[...], wb_ref[...], preferred_element_type=F32).astype(o_ref.dtype)


def _matmul(a, w_all, l, tm, tn, out_dtype=F32, col0=0, n=None):
    m, k = a.shape
    n = w_all.shape[2] if n is None else n
    j0 = col0 // tn
    return pl.pallas_call(
        _mm_kernel, grid=(n // tn, m // tm),
        in_specs=[pl.BlockSpec((tm, k), lambda j, i: (i, 0)),
                  pl.BlockSpec((None, k, tn), lambda j, i: (l, 0, j0 + j))],
        out_specs=pl.BlockSpec((tm, tn), lambda j, i: (i, j)),
        out_shape=jax.ShapeDtypeStruct((m, n), out_dtype),
        scratch_shapes=[pltpu.VMEM((k, tn), BF16)],
        compiler_params=_cparams(2))(a, w_all)


def _merge_kernel(ya_ref, yb_ref, yc_ref, ga_ref, gb_ref, gc_ref, wa_ref, wb_ref, wc_ref, o_ref,
                  wab, wbb, wcb):
    @pl.when(pl.program_id(1) == 0)
    def _():
        wab[...] = wa_ref[...].astype(BF16)
        wbb[...] = wb_ref[...].astype(BF16)
        wcb[...] = wc_ref[...].astype(BF16)

    acc = _sigmoid(ga_ref[...]) * jnp.dot(ya_ref[...], wab[...], preferred_element_type=F32)
    acc = acc + _sigmoid(gb_ref[...]) * jnp.dot(yb_ref[...], wbb[...], preferred_element_type=F32)
    acc = acc + _sigmoid(gc_ref[...]) * jnp.dot(yc_ref[...], wcb[...], preferred_element_type=F32)
    o_ref[...] = acc.astype(o_ref.dtype)


def _merge(ya, yb, yc, proj, wa_all, wb_all, wc_all, l, tm):
    m = ya.shape[0]
    tn = CB
    return pl.pallas_call(
        _merge_kernel, grid=(D_MODEL // tn, m // tm),
        in_specs=[pl.BlockSpec((tm, W_A), lambda j, i: (i, 0)),
                  pl.BlockSpec((tm, W_B), lambda j, i: (i, 0)),
                  pl.BlockSpec((tm, W_C), lambda j, i: (i, 0)),
                  pl.BlockSpec((tm, tn), lambda j, i: (i, COL_GATE_A + j)),
                  pl.BlockSpec((tm, tn), lambda j, i: (i, COL_GATE_B + j)),
                  pl.BlockSpec((tm, tn), lambda j, i: (i, COL_GATE_C + j)),
                  pl.BlockSpec((None, W_A, tn), lambda j, i: (l, 0, j)),
                  pl.BlockSpec((None, W_B, tn), lambda j, i: (l, 0, j)),
                  pl.BlockSpec((None, W_C, tn), lambda j, i: (l, 0, j))],
        out_specs=pl.BlockSpec((tm, tn), lambda j, i: (i, j)),
        out_shape=jax.ShapeDtypeStruct((m, D_MODEL), BF16),
        scratch_shapes=[pltpu.VMEM((W_A, tn), BF16), pltpu.VMEM((W_B, tn), BF16),
                        pltpu.VMEM((W_C, tn), BF16)],
        compiler_params=_cparams(2))(ya, yb, yc, proj, proj, proj, wa_all, wb_all, wc_all)


def _ffn_up_kernel(*refs, tm, shift, halo, tiles_per_seq, has_init):
    if has_init:
        (a_ref, wa_ref, wg_ref, cwa_ref, cwg_ref, cba_ref, cbg_ref, ia_ref, ig_ref,
         act_ref, sta_ref, stg_ref, wab, wgb, ua, ug) = refs
    else:
        (a_ref, wa_ref, wg_ref, cwa_ref, cwg_ref, cba_ref, cbg_ref,
         act_ref, sta_ref, stg_ref, wab, wgb, ua, ug) = refs
    i = pl.program_id(1)
    c = i % tiles_per_seq

    @pl.when(i == 0)
    def _():
        wab[...] = wa_ref[...].astype(BF16)
        wgb[...] = wg_ref[...].astype(BF16)

    if has_init:
        ua[0:halo, :] = ia_ref[...]
        ug[0:halo, :] = ig_ref[...]
    else:
        @pl.when(c == 0)
        def _():
            ua[0:halo, :] = jnp.zeros((halo, ua.shape[1]), F32)
            ug[0:halo, :] = jnp.zeros((halo, ug.shape[1]), F32)

        @pl.when(c > 0)
        def _():
            ua[0:halo, :] = ua[tm:tm + halo, :]
            ug[0:halo, :] = ug[tm:tm + halo, :]

    a = a_ref[...]
    ua[halo:halo + tm, :] = jnp.dot(a, wab[...], preferred_element_type=F32)
    ug[halo:halo + tm, :] = jnp.dot(a, wgb[...], preferred_element_type=F32)

    def conv(u, cw_ref, cb_ref):
        cw = cw_ref[...]
        y = u[halo - 2 * shift:halo - 2 * shift + tm, :] * cw[0:1, :]
        y = y + u[halo - shift:halo - shift + tm, :] * cw[1:2, :]
        y = y + u[halo:halo + tm, :] * cw[2:3, :]
        return y + cb_ref[...]

    act = _gelu_tanh(conv(ug, cwg_ref, cbg_ref)) * conv(ua, cwa_ref, cba_ref)
    act_ref[...] = act.astype(act_ref.dtype)

    ns = (CONV_F - 1) * shift

    @pl.when(c == tiles_per_seq - 1)
    def _():
        sta_ref[...] = ua[halo + tm - ns:halo + tm, :]
        stg_ref[...] = ug[halo + tm - ns:halo + tm, :]


def _ffn_up(a, w_up_all, cw_all, cb_all, l, *, tm, shift, halo, tiles_per_seq, init=None):
    m = a.shape[0]
    tn = CB
    nj = D_FF // tn
    n_seq_tiles = m // tm
    has_init = init is not None
    ns = (CONV_F - 1) * shift
    in_specs = [pl.BlockSpec((tm, D_MODEL), lambda j, i: (i, 0)),
                pl.BlockSpec((None, D_MODEL, tn), lambda j, i: (l, 0, j)),
                pl.BlockSpec((None, D_MODEL, tn), lambda j, i: (l, 0, nj + j)),
                pl.BlockSpec((None, CONV_F, tn), lambda j, i: (l, 0, j)),
                pl.BlockSpec((None, CONV_F, tn), lambda j, i: (l, 0, nj + j)),
                pl.BlockSpec((None, 1, tn), lambda j, i: (l, 0, j)),
                pl.BlockSpec((None, 1, tn), lambda j, i: (l, 0, nj + j))]
    args = [a, w_up_all, w_up_all, cw_all, cw_all, cb_all, cb_all]
    if has_init:
        in_specs += [pl.BlockSpec((halo, tn), lambda j, i: (0, j)),
                     pl.BlockSpec((halo, tn), lambda j, i: (0, nj + j))]
        args += [init, init]
        st_shape = jax.ShapeDtypeStruct((ns, D_FF), F32)
        st_spec = pl.BlockSpec((ns, tn), lambda j, i: (0, j))
    else:
        n_seq = n_seq_tiles // tiles_per_seq
        st_shape = jax.ShapeDtypeStruct((n_seq, ns, D_FF), F32)
        st_spec = pl.BlockSpec((None, ns, tn), lambda j, i: (i // tiles_per_seq, 0, j))
    kern = functools.partial(_ffn_up_kernel, tm=tm, shift=shift, halo=halo,
                             tiles_per_seq=tiles_per_seq, has_init=has_init)
    return pl.pallas_call(
        kern, grid=(nj, n_seq_tiles),
        in_specs=in_specs,
        out_specs=[pl.BlockSpec((tm, tn), lambda j, i: (i, j)), st_spec, st_spec],
        out_shape=[jax.ShapeDtypeStruct((m, D_FF), BF16), st_shape, st_shape],
        scratch_shapes=[pltpu.VMEM((D_MODEL, tn), BF16), pltpu.VMEM((D_MODEL, tn), BF16),
                        pltpu.VMEM((halo + tm, tn), F32), pltpu.VMEM((halo + tm, tn), F32)],
        compiler_params=_cparams(2))(*args)


def _lru_coeffs(xbc, wri_ref, br_ref, bi_ref, lam_ref):
    ri = jnp.dot(xbc.astype(BF16), wri_ref[...], preferred_element_type=F32)
    r = _sigmoid(ri[:, :W_B] + br_ref[...])
    ig = _sigmoid(ri[:, W_B:] + bi_ref[...])
    log_a = (-LRU_C) * r * _softplus(-lam_ref[...])
    a = jnp.exp(log_a)
    mult = jnp.sqrt(1.0 - jnp.exp(2.0 * log_a))
    return a, mult, ig * xbc


def _bc_prompt_kernel(xb_ref, xc_ref, gb_ref, gc_ref, cw_ref, cb_ref, wri_ref, br_ref, bi_ref,
                      lam_ref, sw_ref, yb_ref, yc_ref, lbuf_ref, hlast_ref, sbuf_ref,
                      xs, us, a_s, b_s, h_s, hprev, *, tm, tiles_per_seq):
    c = pl.program_id(1)
    first = c == 0

    @pl.when(first)
    def _():
        xs[0:8, :] = jnp.zeros((8, W_B), F32)
        us[0:8, :] = jnp.zeros((8, W_C), F32)
        hprev[...] = jnp.zeros((8, W_B), F32)

    @pl.when(c > 0)
    def _():
        xs[0:8, :] = xs[tm:tm + 8, :]
        us[0:8, :] = us[tm:tm + 8, :]

    xs[8:8 + tm, :] = xb_ref[...]
    us[8:8 + tm, :] = gc_ref[...] * xc_ref[...]

    cw = cw_ref[...]
    xbc = cb_ref[...] + xs[8:8 + tm, :] * cw[3:4, :]
    for i in range(CONV_B - 1):
        xbc = xbc + xs[5 + i:5 + i + tm, :] * cw[i:i + 1, :]
    a, mult, ix = _lru_coeffs(xbc, wri_ref, br_ref, bi_ref, lam_ref)
    a_s[...] = a
    b_s[...] = mult * ix

    @pl.when(first)
    def _():
        b_s[0:1, :] = ix[0:1, :]

    row8 = lax.broadcasted_iota(jnp.int32, (8, W_B), 0)

    def tile_body(t, hp):
        off = pl.multiple_of(t * 8, 8)
        av = a_s[pl.ds(off, 8), :]
        bv = b_s[pl.ds(off, 8), :]
        for k in (1, 2, 4):
            a_sh = jnp.where(row8 >= k, pltpu.roll(av, k, 0), 1.0)
            b_sh = jnp.where(row8 >= k, pltpu.roll(bv, k, 0), 0.0)
            bv = av * b_sh + bv
            av = av * a_sh
        hv = av * hp + bv
        h_s[pl.ds(off, 8), :] = hv
        return jnp.broadcast_to(hv[7:8, :], (8, W_B))

    hprev[...] = lax.fori_loop(0, tm // 8, tile_body, hprev[...])
    yb_ref[...] = h_s[...].astype(yb_ref.dtype)

    sw = sw_ref[...]
    yc = us[6:6 + tm, :] * sw[0:1, :] + us[7:7 + tm, :] * sw[1:2, :] + us[8:8 + tm, :] * sw[2:3, :]
    yc_ref[...] = (gb_ref[...] * yc).astype(yc_ref.dtype)

    @pl.when(c == tiles_per_seq - 1)
    def _():
        lbuf_ref[...] = xs[8 + tm - (CONV_B - 1):8 + tm, :]
        hlast_ref[...] = h_s[tm - 1:tm, :]
        sbuf_ref[...] = us[8 + tm - (CONV_C - 1):8 + tm, :]


def _bc_prompt(proj, n_seq, tm, tiles_per_seq, cw_all, cb_all, wri_all, br_all, bi_all, lam_all,
               sw_all, l):
    m = proj.shape[0]
    col = lambda cidx: pl.BlockSpec((tm, CB), lambda b, c: (b * tiles_per_seq + c, cidx))
    par = lambda rows, width: pl.BlockSpec((None, rows, width), lambda b, c: (l, 0, 0))
    seq_out = lambda rows: pl.BlockSpec((None, rows, CB), lambda b, c: (b, 0, 0))
    row_out = pl.BlockSpec((tm, CB), lambda b, c: (b * tiles_per_seq + c, 0))
    kern = functools.partial(_bc_prompt_kernel, tm=tm, tiles_per_seq=tiles_per_seq)
    return pl.pallas_call(
        kern, grid=(n_seq, tiles_per_seq),
        in_specs=[col(COL_XB), col(COL_XC), col(COL_GB), col(COL_GC),
                  par(CONV_B, W_B), par(1, W_B), par(W_B, 2 * W_B), par(1, W_B), par(1, W_B),
                  par(1, W_B), par(CONV_C, W_C)],
        out_specs=[row_out, row_out, seq_out(CONV_B - 1), seq_out(1), seq_out(CONV_C - 1)],
        out_shape=[jax.ShapeDtypeStruct((m, W_B), BF16), jax.ShapeDtypeStruct((m, W_C), BF16),
                   jax.ShapeDtypeStruct((n_seq, CONV_B - 1, W_B), F32),
                   jax.ShapeDtypeStruct((n_seq, 1, W_B), F32),
                   jax.ShapeDtypeStruct((n_seq, CONV_C - 1, W_C), F32)],
        scratch_shapes=[pltpu.VMEM((8 + tm, W_B), F32), pltpu.VMEM((8 + tm, W_C), F32),
                        pltpu.VMEM((tm, W_B), F32), pltpu.VMEM((tm, W_B), F32),
                        pltpu.VMEM((tm, W_B), F32), pltpu.VMEM((8, W_B), F32)],
        compiler_params=_cparams(2))(proj, proj, proj, proj, cw_all, cb_all, wri_all, br_all,
                                     bi_all, lam_all, sw_all)


def _bc_sample_kernel(xb_ref, xc_ref, gb_ref, gc_ref, lbuf_in, h0_ref, sbuf_in, cw_ref, cb_ref,
                      wri_ref, br_ref, bi_ref, lam_ref, sw_ref, yb_ref, yc_ref, lbuf_ref, hlast_ref,
                      sbuf_ref, xs, us, *, nb, nt):
    m = nb * nt
    hb = (CONV_B - 1) * nb
    hc = (CONV_C - 1) * nb
    xs[0:hb, :] = lbuf_in[...]
    xs[hb:hb + m, :] = xb_ref[...]
    us[0:hc, :] = sbuf_in[...]
    us[hc:hc + m, :] = gc_ref[...] * xc_ref[...]

    cw = cw_ref[...]
    xbc = cb_ref[...] + xs[hb:hb + m, :] * cw[3:4, :]
    for i in range(CONV_B - 1):
        xbc = xbc + xs[i * nb:i * nb + m, :] * cw[i:i + 1, :]
    a, mult, ix = _lru_coeffs(xbc, wri_ref, br_ref, bi_ref, lam_ref)
    bterm = mult * ix
    h = h0_ref[...]
    for t in range(nt):
        h = a[t * nb:(t + 1) * nb, :] * h + bterm[t * nb:(t + 1) * nb, :]
        yb_ref[t * nb:(t + 1) * nb, :] = h.astype(yb_ref.dtype)
    hlast_ref[...] = h
    lbuf_ref[...] = xs[m:m + hb, :]

    sw = sw_ref[...]
    yc = us[0:m, :] * sw[0:1, :] + us[nb:nb + m, :] * sw[1:2, :] + us[2 * nb:2 * nb + m, :] * sw[2:3, :]
    yc_ref[...] = (gb_ref[...] * yc).astype(yc_ref.dtype)
    sbuf_ref[...] = us[m:m + hc, :]


def _bc_sample(proj, nb, nt, lbuf, h0, sbuf, cw_all, cb_all, wri_all, br_all, bi_all, lam_all,
               sw_all, l):
    m = nb * nt
    hb = (CONV_B - 1) * nb
    hc = (CONV_C - 1) * nb
    col = lambda cidx: pl.BlockSpec((m, CB), lambda i: (0, cidx))
    full = lambda rows: pl.BlockSpec((rows, CB), lambda i: (0, 0))
    par = lambda rows, width: pl.BlockSpec((None, rows, width), lambda i: (l, 0, 0))
    kern = functools.partial(_bc_sample_kernel, nb=nb, nt=nt)
    return pl.pallas_call(
        kern, grid=(1,),
        in_specs=[col(COL_XB), col(COL_XC), col(COL_GB), col(COL_GC), full(hb), full(nb), full(hc),
                  par(CONV_B, W_B), par(1, W_B), par(W_B, 2 * W_B), par(1, W_B), par(1, W_B),
                  par(1, W_B), par(CONV_C, W_C)],
        out_specs=[full(m), full(m), full(hb), full(nb), full(hc)],
        out_shape=[jax.ShapeDtypeStruct((m, W_B), BF16), jax.ShapeDtypeStruct((m, W_C), BF16),
                   jax.ShapeDtypeStruct((hb, W_B), F32), jax.ShapeDtypeStruct((nb, W_B), F32),
                   jax.ShapeDtypeStruct((hc, W_C), F32)],
        scratch_shapes=[pltpu.VMEM((hb + m, W_B), F32), pltpu.VMEM((hc + m, W_C), F32)],
        compiler_params=_cparams(1))(proj, proj, proj, proj, lbuf, h0, sbuf, cw_all, cb_all,
                                     wri_all, br_all, bi_all, lam_all, sw_all)


def _sb_block(qt, kblk, vblk, tri, bias, carry, acc, mask):
    z = lax.dot_general(qt, kblk, (((1,), (1,)), ((), ())), preferred_element_type=F32) + bias
    sp = _softplus(z)
    lk = -sp if mask is None else jnp.where(mask, -sp, 0.0)
    btw = (jnp.dot(lk.astype(BF16), tri, preferred_element_type=F32)
           + jnp.tile(carry, (1, lk.shape[1] // DH)))
    w = jnp.exp(z - sp + btw)
    if mask is not None:
        w = jnp.where(mask, w, 0.0)
    acc = acc + jnp.dot(w.astype(BF16), vblk, preferred_element_type=F32)
    carry = carry + jnp.broadcast_to(jnp.sum(lk, axis=1, keepdims=True), carry.shape)
    return acc, carry


def _attn_prompt_kernel(bias_ref, q_ref, k_ref, v_ref, tri_k_ref, tri_m_ref, o_ref, qb, kb, vb, *,
                        layer, tq, tk, n_real, unroll):
    bias = bias_ref[layer, pl.program_id(1)]
    qb[...] = (q_ref[...] * SCALE).astype(BF16)
    kb[...] = k_ref[...].astype(BF16)
    vb[...] = v_ref[...].astype(BF16)
    tri_k = tri_k_ref[...]
    tri_m = tri_m_ref[...]
    meta_mask = lax.broadcasted_iota(jnp.int32, (tq, PAGE), 1) < N_META
    kpq = tq // tk

    for qi in range(n_real // tq):
        q0 = N_META + tq * qi
        qt = qb[q0:q0 + tq, :]
        acc = jnp.zeros((tq, DH), F32)
        carry = jnp.zeros((tq, DH), F32)
        for d in reversed(range(kpq)):
            k0 = q0 + d * tk
            r0 = d * tk
            vis = (lax.broadcasted_iota(jnp.int32, (tq - r0, tk), 1)
                   < lax.broadcasted_iota(jnp.int32, (tq - r0, tk), 0))
            a_new, c_new = _sb_block(qt[r0:, :], kb[k0:k0 + tk, :], vb[k0:k0 + tk, :], tri_k, bias,
                                     carry[r0:, :], acc[r0:, :], vis)
            acc = a_new if r0 == 0 else jnp.concatenate([acc[:r0, :], a_new], axis=0)
            carry = c_new if r0 == 0 else jnp.concatenate([carry[:r0, :], c_new], axis=0)
        n_low = qi * kpq
        if n_low > 0:
            def body(jj, st, n_low=n_low, qt=qt):
                off = pl.multiple_of(N_META + (n_low - 1 - jj) * tk, 16)
                return _sb_block(qt, kb[pl.ds(off, tk), :], vb[pl.ds(off, tk), :], tri_k, bias,
                                 st[1], st[0], None)

            acc, carry = lax.fori_loop(0, n_low, body, (acc, carry), unroll=unroll)
        acc, carry = _sb_block(qt, kb[0:PAGE, :], vb[0:PAGE, :], tri_m, bias, carry, acc, meta_mask)
        o_ref[q0:q0 + tq, :] = acc.astype(o_ref.dtype)

    mm = (lax.broadcasted_iota(jnp.int32, (N_META, PAGE), 1)
          < lax.broadcasted_iota(jnp.int32, (N_META, PAGE), 0))
    acc, _ = _sb_block(qb[0:N_META, :], kb[0:PAGE, :], vb[0:PAGE, :], tri_m, bias,
                       jnp.zeros((N_META, DH), F32), jnp.zeros((N_META, DH), F32), mm)
    o_ref[0:N_META, :] = acc.astype(o_ref.dtype)


def _later_key_matrix(n):
    return (lax.broadcasted_iota(jnp.int32, (n, n), 0)
            > lax.broadcasted_iota(jnp.int32, (n, n), 1)).astype(BF16)


def _attn_prompt(q, k, v, sb_bias, l, n_seq, t_seq, tq, tk, unroll):
    m = q.shape[0]
    kern = functools.partial(_attn_prompt_kernel, layer=l, tq=tq, tk=tk, n_real=t_seq - N_META,
                             unroll=unroll)
    head = pl.BlockSpec((t_seq, DH), lambda b, h: (b, h))
    const = lambda n: pl.BlockSpec((n, n), lambda b, h: (0, 0))
    return pl.pallas_call(
        kern, grid=(n_seq, H_A),
        in_specs=[pl.BlockSpec(memory_space=pltpu.SMEM), head, head, head, const(tk), const(PAGE)],
        out_specs=head,
        out_shape=jax.ShapeDtypeStruct((m, W_A), BF16),
        scratch_shapes=[pltpu.VMEM((t_seq, DH), BF16)] * 3,
        compiler_params=_cparams(2))(sb_bias, q, k, v, _later_key_matrix(tk),
                                     _later_key_matrix(PAGE))


def _attn_sample_kernel(*refs, nt, pps, cpc):
    n_pg = 2 * pps
    q_ref, kn_ref, vn_ref = refs[1:4]
    k_refs = refs[4:4 + n_pg]
    v_refs = refs[4 + n_pg:4 + 2 * n_pg]
    bias_ref, tri_ref, o_ref, qf, qbd, kpad, vpad, k2s, v2s, acc, carry = refs[4 + 2 * n_pg:]
    j = pl.program_id(1)
    lanes = 2 * H_A * nt
    wk = 2 * W_A

    def step(k2, v2, tri, mask, a, c):
        z = jnp.dot(k2, qbd[...], preferred_element_type=F32) * SCALE + bias_ref[...]
        sp = _softplus(z)
        lk = -sp if mask is None else jnp.where(mask, -sp, 0.0)
        bt = jnp.dot(tri, lk.astype(BF16), preferred_element_type=F32)
        w = jnp.exp(z - sp + bt + c)
        if mask is not None:
            w = jnp.where(mask, w, 0.0)
        a = a + jnp.dot(w.T.astype(BF16), v2, preferred_element_type=F32)
        return a, c + jnp.sum(lk, axis=0, keepdims=True)

    @pl.when(j == 0)
    def _():
        @pl.when(pl.program_id(0) == 0)
        def _():
            qf[...] = jnp.zeros((lanes, wk), F32)
            kpad[...] = jnp.zeros((PAGE, wk), F32)
            vpad[...] = jnp.zeros((PAGE, wk), F32)

        for b in range(2):
            kpad[0:nt, b * W_A:(b + 1) * W_A] = kn_ref[b]
            vpad[0:nt, b * W_A:(b + 1) * W_A] = vn_ref[b]
            for h in range(H_A):
                r0 = (b * H_A + h) * nt
                c0 = b * W_A + h * DH
                qf[r0:r0 + nt, c0:c0 + DH] = q_ref[b, :, h * DH:(h + 1) * DH]
        qbd[...] = qf[...].T.astype(BF16)
        row = lax.broadcasted_iota(jnp.int32, (PAGE, lanes), 0)
        tq = lax.broadcasted_iota(jnp.int32, (PAGE, lanes), 1) % nt
        acc[...], carry[...] = step(kpad[...].astype(BF16), vpad[...].astype(BF16),
                                    tri_ref[0:PAGE, 0:PAGE], row < tq,
                                    jnp.zeros((lanes, wk), F32), jnp.zeros((1, lanes), F32))

    for b in range(2):
        for i in range(pps):
            for h in range(H_A):
                c0 = b * W_A + h * DH
                k2s[i * PAGE:(i + 1) * PAGE, c0:c0 + DH] = (
                    k_refs[b * pps + i][pl.ds(h, PAGE, stride=H_A), :].astype(BF16))
                v2s[i * PAGE:(i + 1) * PAGE, c0:c0 + DH] = (
                    v_refs[b * pps + i][pl.ds(h, PAGE, stride=H_A), :].astype(BF16))
    rc = cpc * PAGE
    a, c = acc[...], carry[...]
    for ch in reversed(range(pps // cpc)):
        a, c = step(k2s[ch * rc:(ch + 1) * rc, :], v2s[ch * rc:(ch + 1) * rc, :], tri_ref[...], None,
                    a, c)
    acc[...], carry[...] = a, c

    @pl.when(j == pl.num_programs(1) - 1)
    def _():
        for b in range(2):
            for h in range(H_A):
                r0 = (b * H_A + h) * nt
                c0 = b * W_A + h * DH
                o_ref[b, :, h * DH:(h + 1) * DH] = acc[r0:r0 + nt, c0:c0 + DH]


def _attn_sample(q, kn, vn, cache_k, cache_v, page_table, bias_lanes, l, pps=4, cpc=2):
    nb, nt, _ = q.shape
    n_pages = page_table.shape[1]
    pt = page_table.reshape(-1)
    lanes = 2 * H_A * nt
    wk = 2 * W_A
    rows = pps * PAGE
    tok = pl.BlockSpec((2, nt, W_A), lambda p, j, pt_ref: (p, 0, 0))

    def page(which, i):
        def index_map(p, j, pt_ref):
            return (l, pt_ref[(2 * p + which) * n_pages + n_pages - (j + 1) * pps + i], 0, 0)
        return pl.BlockSpec((None, None, PAGE * H_A, DH), index_map)

    pages = [page(b, i) for b in range(2) for i in range(pps)]
    kern = functools.partial(_attn_sample_kernel, nt=nt, pps=pps, cpc=cpc)
    grid_spec = pltpu.PrefetchScalarGridSpec(
        num_scalar_prefetch=1, grid=(nb // 2, n_pages // pps),
        in_specs=[tok, tok, tok] + pages + pages + [
            pl.BlockSpec((1, lanes), lambda p, j, pt_ref: (0, 0)),
            pl.BlockSpec((cpc * PAGE, cpc * PAGE), lambda p, j, pt_ref: (0, 0))],
        out_specs=tok,
        scratch_shapes=[pltpu.VMEM((lanes, wk), F32), pltpu.VMEM((wk, lanes), BF16),
                        pltpu.VMEM((PAGE, wk), F32), pltpu.VMEM((PAGE, wk), F32),
                        pltpu.VMEM((rows, wk), BF16), pltpu.VMEM((rows, wk), BF16),
                        pltpu.VMEM((lanes, wk), F32), pltpu.VMEM((1, lanes), F32)])
    tri = _later_key_matrix(cpc * PAGE).T
    return pl.pallas_call(
        kern, grid_spec=grid_spec, out_shape=jax.ShapeDtypeStruct((nb, nt, W_A), F32),
        compiler_params=_cparams(2))(pt, q, kn, vn, *([cache_k] * (2 * pps)),
                                     *([cache_v] * (2 * pps)), bias_lanes, tri)


def _block_diag(w):
    depth, nblk, c, d = w.shape
    eye = jnp.eye(nblk, dtype=w.dtype)
    return jnp.einsum('lncd,nm->lncmd', w, eye).reshape(depth, nblk * c, nblk * d)


def kernel(x_prompt, x_sample, cache_k, cache_v, state_lru_conv, state_lru_h, state_sconv, state_ffn_conv, page_table, meta_tokens, norm_mix_pre, norm_mix_post, norm_ffn_pre, norm_ffn_post, w_in, sb_bias, lru_conv_w, lru_conv_b, lru_w_r, lru_b_r, lru_w_i, lru_b_i, lru_lambda, sconv_w, w_out_a, w_out_b, w_out_c, w_o, ffn_w_up, ffn_conv_w, ffn_conv_b, ffn_w_down):
    bp, seq, _ = x_prompt.shape
    nb, nt, _ = x_sample.shape
    t_seq = N_META + seq
    n_pool = cache_k.shape[1]

    row3 = lambda p: p.reshape(DEPTH, 1, p.shape[-1])
    g_mix_pre, g_mix_post = row3(norm_mix_pre), row3(norm_mix_post)
    g_ffn_pre, g_ffn_post = row3(norm_ffn_pre), row3(norm_ffn_post)
    lcb, lbr, lbi, lam = row3(lru_conv_b), row3(lru_b_r), row3(lru_b_i), row3(lru_lambda)
    fcb = row3(ffn_conv_b)
    wri = jnp.concatenate([_block_diag(lru_w_r), _block_diag(lru_w_i)], axis=-1).astype(BF16)
    ck = cache_k.reshape(DEPTH, n_pool, PAGE * H_A, DH)
    cv = cache_v.reshape(DEPTH, n_pool, PAGE * H_A, DH)

    tp = t_seq // 3
    ms = nb * nt

    def layer(x, xn, l, prompt):
        last = l == DEPTH - 1
        if prompt:
            tm_big, tm, tm_row = t_seq, tp, tp
        else:
            tm_big, tm, tm_row = ms, ms, ms // 4
        q = _matmul(xn, w_in, l, tm_big, CB, col0=0, n=W_A)
        k = _matmul(xn, w_in, l, tm_big, CB, col0=W_A, n=W_A)
        v = _matmul(xn, w_in, l, tm_big, CB, col0=2 * W_A, n=W_A)
        proj = _matmul(xn, w_in, l, tm_big, CB, col0=3 * W_A, n=N_IN - 3 * W_A)
        if prompt:
            ya = _attn_prompt(q, k, v, sb_bias, l, bp, t_seq, 1024, 256, 2)
            yb, yc, lbuf, hlast, sbuf = _bc_prompt(proj, bp, tm, 3, lru_conv_w, lcb, wri, lbr, lbi,
                                                   lam, sconv_w, l)
            hlast = hlast.reshape(bp, W_B)
            k_out = k.reshape(bp, t_seq, H_A, DH)
            v_out = v.reshape(bp, t_seq, H_A, DH)
        else:
            to_bm = lambda a: a.reshape(-1, nb, a.shape[-1]).transpose(1, 0, 2)
            to_tm = lambda a: a.transpose(1, 0, 2).reshape(a.shape[1] * nb, a.shape[-1])
            q_bm, k_bm, v_bm = to_bm(q), to_bm(k), to_bm(v)
            bias_lanes = jnp.tile(jnp.repeat(sb_bias[l], nt), 2).reshape(1, 2 * H_A * nt)
            ya = to_tm(_attn_sample(q_bm, k_bm, v_bm, ck, cv, page_table, bias_lanes, l,
                                    pps=8)).astype(BF16)
            yb, yc, lbuf, hlast, sbuf = _bc_sample(
                proj, nb, nt, to_tm(state_lru_conv[l]), state_lru_h[l], to_tm(state_sconv[l]),
                lru_conv_w, lcb, wri, lbr, lbi, lam, sconv_w, l)
            lbuf, sbuf = to_bm(lbuf), to_bm(sbuf)
            k_out = k_bm.reshape(nb, nt, H_A, DH)
            v_out = v_bm.reshape(nb, nt, H_A, DH)
        merged = _merge(ya, yb, yc, proj, w_out_a, w_out_b, w_out_c, l, tm)
        mo = _matmul(merged, w_o, l, tm_big, CB)
        x, xn = _resid_norm(x, mo, g_mix_post, l, g_ffn_pre, l, tm_row)
        if prompt:
            act, sta, stg = _ffn_up(xn, ffn_w_up, ffn_conv_w, fcb, l, tm=tm, shift=1, halo=8,
                                    tiles_per_seq=3)
            fbuf = jnp.concatenate([sta, stg], axis=-1)
        else:
            act, sta, stg = _ffn_up(xn, ffn_w_up, ffn_conv_w, fcb, l, tm=tm, shift=nb,
                                    halo=(CONV_F - 1) * nb, tiles_per_seq=1,
                                    init=to_tm(state_ffn_conv[l]))
            fbuf = to_bm(jnp.concatenate([sta, stg], axis=-1))
        f = _matmul(act, ffn_w_down, l, tm, CB if prompt else CB // 2)
        x, xn = _resid_norm(x, f, g_ffn_post, l, None if last else g_mix_pre, l + 1, tm_row)
        return x, xn, (k_out, v_out, lbuf, hlast, sbuf, fbuf)

    def run(x, prompt, tm):
        xn = _rmsnorm_cast(x, g_mix_pre, 0, tm)
        outs = []
        for l in range(DEPTH):
            x, xn, o = layer(x, xn, l, prompt)
            outs.append(o)
        return x, [jnp.stack(z) for z in zip(*outs)]

    dt = x_prompt.dtype
    xp = jnp.concatenate([jnp.broadcast_to(meta_tokens.astype(dt)[None], (bp, N_META, D_MODEL)),
                          x_prompt], axis=1).reshape(bp * t_seq, D_MODEL)
    xp, (kp, vp, lbp, hlp, sbp, fbp) = run(xp, True, tp)
    y_prompt = xp.reshape(bp, t_seq, D_MODEL)[:, N_META:]

    xs = x_sample.transpose(1, 0, 2).reshape(ms, D_MODEL)
    xs, (ks, vs, lbs, hls, sbs, fbs) = run(xs, False, ms // 4)
    y_sample = xs.reshape(nt, nb, D_MODEL).transpose(1, 0, 2)

    return (y_prompt, y_sample, kp, vp, ks, vs, lbp, lbs, hlp, hls, sbp, sbs, fbp, fbs)
```

```python
import functools

import jax
import jax.numpy as jnp
from jax import lax
from jax.experimental import pallas as pl
from jax.experimental.pallas import tpu as pltpu

F32 = jnp.float32
BF16 = jnp.bfloat16

D_MODEL = 2048
DEPTH = 2
N_META = 16
H_A = 8
DH = 128
W_A = H_A * DH
W_B = 512
NB_LRU = 8
W_C = 512
CONV_B = 4
CONV_C = 3
CONV_F = 3
D_FF = 5632
N_BRANCH = 3
LRU_C = 8.0
EPS = 1e-6
SCALE = DH ** -0.5
N_IN = 3 * W_A + W_B + 3 * W_C + N_BRANCH * D_MODEL
PAGE = 128

CB = 512
COL_XB, COL_XC, COL_GB, COL_GC = 0, 1, 2, 3
COL_GATE_A, COL_GATE_B, COL_GATE_C = 4, 8, 12

VMEM_LIMIT_BYTES = 58 * 1024 * 1024


def _cparams(n_axes):
    return pltpu.CompilerParams(dimension_semantics=("arbitrary",) * n_axes,
                                vmem_limit_bytes=VMEM_LIMIT_BYTES)


def _sigmoid(x):
    return 1.0 / (1.0 + jnp.exp(-x))


def _softplus(x):
    return jnp.maximum(x, 0.0) + jnp.log(1.0 + jnp.exp(-jnp.abs(x)))


def _gelu_tanh(g):
    return 0.5 * g * (1.0 + jnp.tanh(0.7978845608028654 * (g + 0.044715 * (g * g * g))))


def _rms(x, g):
    return x * lax.rsqrt(jnp.mean(x * x, axis=-1, keepdims=True) + EPS) * g


def _rmsnorm_kernel(x_ref, g_ref, o_ref):
    o_ref[...] = _rms(x_ref[...], g_ref[...]).astype(o_ref.dtype)


def _rmsnorm_cast(x, g_all, l, tm):
    m = x.shape[0]
    return pl.pallas_call(
        _rmsnorm_kernel, grid=(m // tm,),
        in_specs=[pl.BlockSpec((tm, D_MODEL), lambda i: (i, 0)),
                  pl.BlockSpec((None, 1, D_MODEL), lambda i: (l, 0, 0))],
        out_specs=pl.BlockSpec((tm, D_MODEL), lambda i: (i, 0)),
        out_shape=jax.ShapeDtypeStruct((m, D_MODEL), BF16),
        compiler_params=_cparams(1))(x, g_all)


def _resid_norm_kernel(x_ref, m_ref, gpost_ref, gnext_ref, xo_ref, xn_ref):
    x1 = x_ref[...] + _rms(m_ref[...], gpost_ref[...])
    xo_ref[...] = x1
    xn_ref[...] = _rms(x1, gnext_ref[...]).astype(xn_ref.dtype)


def _resid_kernel(x_ref, m_ref, gpost_ref, xo_ref):
    xo_ref[...] = x_ref[...] + _rms(m_ref[...], gpost_ref[...])


def _resid_norm(x, m_in, gpost_all, l, gnext_all, l_next, tm):
    m = x.shape[0]
    row = pl.BlockSpec((tm, D_MODEL), lambda i: (i, 0))
    if gnext_all is None:
        return pl.pallas_call(
            _resid_kernel, grid=(m // tm,),
            in_specs=[row, row, pl.BlockSpec((None, 1, D_MODEL), lambda i: (l, 0, 0))],
            out_specs=row, out_shape=jax.ShapeDtypeStruct((m, D_MODEL), F32),
            compiler_params=_cparams(1))(x, m_in, gpost_all), None
    return pl.pallas_call(
        _resid_norm_kernel, grid=(m // tm,),
        in_specs=[row, row, pl.BlockSpec((None, 1, D_MODEL), lambda i: (l, 0, 0)),
                  pl.BlockSpec((None, 1, D_MODEL), lambda i: (l_next, 0, 0))],
        out_specs=[row, row],
        out_shape=[jax.ShapeDtypeStruct((m, D_MODEL), F32), jax.ShapeDtypeStruct((m, D_MODEL), BF16)],
        compiler_params=_cparams(1))(x, m_in, gpost_all, gnext_all)


def _mm_kernel(a_ref, w_ref, o_ref, wb_ref):
    @pl.when(pl.program_id(1) == 0)
    def _():
        wb_ref[...] = w_ref[...].astype(BF16)

    o_ref[...] = jnp.dot(a_ref[...], wb_ref[...], preferred_element_type=F32).astype(o_ref.dtype)


def _matmul(a, w_all, l, tm, tn, out_dtype=F32, col0=0, n=None):
    m, k = a.shape
    n = w_all.shape[2] if n is None else n
    j0 = col0 // tn
    return pl.pallas_call(
        _mm_kernel, grid=(n // tn, m // tm),
        in_specs=[pl.BlockSpec((tm, k), lambda j, i: (i, 0)),
                  pl.BlockSpec((None, k, tn), lambda j, i: (l, 0, j0 + j))],
        out_specs=pl.BlockSpec((tm, tn), lambda j, i: (i, j)),
        out_shape=jax.ShapeDtypeStruct((m, n), out_dtype),
        scratch_shapes=[pltpu.VMEM((k, tn), BF16)],
        compiler_params=_cparams(2))(a, w_all)


def _merge_kernel(ya_ref, yb_ref, yc_ref, ga_ref, gb_ref, gc_ref, wa_ref, wb_ref, wc_ref, o_ref,
                  wab, wbb, wcb):
    @pl.when(pl.program_id(1) == 0)
    def _():
        wab[...] = wa_ref[...].astype(BF16)
        wbb[...] = wb_ref[...].astype(BF16)
        wcb[...] = wc_ref[...].astype(BF16)

    acc = _sigmoid(ga_ref[...]) * jnp.dot(ya_ref[...], wab[...], preferred_element_type=F32)
    acc = acc + _sigmoid(gb_ref[...]) * jnp.dot(yb_ref[...], wbb[...], preferred_element_type=F32)
    acc = acc + _sigmoid(gc_ref[...]) * jnp.dot(yc_ref[...], wcb[...], preferred_element_type=F32)
    o_ref[...] = acc.astype(o_ref.dtype)


def _merge(ya, yb, yc, proj, wa_all, wb_all, wc_all, l, tm):
    m = ya.shape[0]
    tn = CB
    return pl.pallas_call(
        _merge_kernel, grid=(D_MODEL // tn, m // tm),
        in_specs=[pl.BlockSpec((tm, W_A), lambda j, i: (i, 0)),
                  pl.BlockSpec((tm, W_B), lambda j, i: (i, 0)),
                  pl.BlockSpec((tm, W_C), lambda j, i: (i, 0)),
                  pl.BlockSpec((tm, tn), lambda j, i: (i, COL_GATE_A + j)),
                  pl.BlockSpec((tm, tn), lambda j, i: (i, COL_GATE_B + j)),
                  pl.BlockSpec((tm, tn), lambda j, i: (i, COL_GATE_C + j)),
                  pl.BlockSpec((None, W_A, tn), lambda j, i: (l, 0, j)),
                  pl.BlockSpec((None, W_B, tn), lambda j, i: (l, 0, j)),
                  pl.BlockSpec((None, W_C, tn), lambda j, i: (l, 0, j))],
        out_specs=pl.BlockSpec((tm, tn), lambda j, i: (i, j)),
        out_shape=jax.ShapeDtypeStruct((m, D_MODEL), BF16),
        scratch_shapes=[pltpu.VMEM((W_A, tn), BF16), pltpu.VMEM((W_B, tn), BF16),
                        pltpu.VMEM((W_C, tn), BF16)],
        compiler_params=_cparams(2))(ya, yb, yc, proj, proj, proj, wa_all, wb_all, wc_all)


def _ffn_up_kernel(*refs, tm, shift, halo, tiles_per_seq, has_init):
    if has_init:
        (a_ref, wa_ref, wg_ref, cwa_ref, cwg_ref, cba_ref, cbg_ref, ia_ref, ig_ref,
         act_ref, sta_ref, stg_ref, wab, wgb, ua, ug) = refs
    else:
        (a_ref, wa_ref, wg_ref, cwa_ref, cwg_ref, cba_ref, cbg_ref,
         act_ref, sta_ref, stg_ref, wab, wgb, ua, ug) = refs
    i = pl.program_id(1)
    c = i % tiles_per_seq

    @pl.when(i == 0)
    def _():
        wab[...] = wa_ref[...].astype(BF16)
        wgb[...] = wg_ref[...].astype(BF16)

    if has_init:
        ua[0:halo, :] = ia_ref[...]
        ug[0:halo, :] = ig_ref[...]
    else:
        @pl.when(c == 0)
        def _():
            ua[0:halo, :] = jnp.zeros((halo, ua.shape[1]), F32)
            ug[0:halo, :] = jnp.zeros((halo, ug.shape[1]), F32)

        @pl.when(c > 0)
        def _():
            ua[0:halo, :] = ua[tm:tm + halo, :]
            ug[0:halo, :] = ug[tm:tm + halo, :]

    a = a_ref[...]
    ua[halo:halo + tm, :] = jnp.dot(a, wab[...], preferred_element_type=F32)
    ug[halo:halo + tm, :] = jnp.dot(a, wgb[...], preferred_element_type=F32)

    def conv(u, cw_ref, cb_ref):
        cw = cw_ref[...]
        y = u[halo - 2 * shift:halo - 2 * shift + tm, :] * cw[0:1, :]
        y = y + u[halo - shift:halo - shift + tm, :] * cw[1:2, :]
        y = y + u[halo:halo + tm, :] * cw[2:3, :]
        return y + cb_ref[...]

    act = _gelu_tanh(conv(ug, cwg_ref, cbg_ref)) * conv(ua, cwa_ref, cba_ref)
    act_ref[...] = act.astype(act_ref.dtype)

    ns = (CONV_F - 1) * shift

    @pl.when(c == tiles_per_seq - 1)
    def _():
        sta_ref[...] = ua[halo + tm - ns:halo + tm, :]
        stg_ref[...] = ug[halo + tm - ns:halo + tm, :]


def _ffn_up(a, w_up_all, cw_all, cb_all, l, *, tm, shift, halo, tiles_per_seq, init=None):
    m = a.shape[0]
    tn = CB
    nj = D_FF // tn
    n_seq_tiles = m // tm
    has_init = init is not None
    ns = (CONV_F - 1) * shift
    in_specs = [pl.BlockSpec((tm, D_MODEL), lambda j, i: (i, 0)),
                pl.BlockSpec((None, D_MODEL, tn), lambda j, i: (l, 0, j)),
                pl.BlockSpec((None, D_MODEL, tn), lambda j, i: (l, 0, nj + j)),
                pl.BlockSpec((None, CONV_F, tn), lambda j, i: (l, 0, j)),
                pl.BlockSpec((None, CONV_F, tn), lambda j, i: (l, 0, nj + j)),
                pl.BlockSpec((None, 1, tn), lambda j, i: (l, 0, j)),
                pl.BlockSpec((None, 1, tn), lambda j, i: (l, 0, nj + j))]
    args = [a, w_up_all, w_up_all, cw_all, cw_all, cb_all, cb_all]
    if has_init:
        in_specs += [pl.BlockSpec((halo, tn), lambda j, i: (0, j)),
                     pl.BlockSpec((halo, tn), lambda j, i: (0, nj + j))]
        args += [init, init]
        st_shape = jax.ShapeDtypeStruct((ns, D_FF), F32)
        st_spec = pl.BlockSpec((ns, tn), lambda j, i: (0, j))
    else:
        n_seq = n_seq_tiles // tiles_per_seq
        st_shape = jax.ShapeDtypeStruct((n_seq, ns, D_FF), F32)
        st_spec = pl.BlockSpec((None, ns, tn), lambda j, i: (i // tiles_per_seq, 0, j))
    kern = functools.partial(_ffn_up_kernel, tm=tm, shift=shift, halo=halo,
                             tiles_per_seq=tiles_per_seq, has_init=has_init)
    return pl.pallas_call(
        kern, grid=(nj, n_seq_tiles),
        in_specs=in_specs,
        out_specs=[pl.BlockSpec((tm, tn), lambda j, i: (i, j)), st_spec, st_spec],
        out_shape=[jax.ShapeDtypeStruct((m, D_FF), BF16), st_shape, st_shape],
        scratch_shapes=[pltpu.VMEM((D_MODEL, tn), BF16), pltpu.VMEM((D_MODEL, tn), BF16),
                        pltpu.VMEM((halo + tm, tn), F32), pltpu.VMEM((halo + tm, tn), F32)],
        compiler_params=_cparams(2))(*args)


def _lru_coeffs(xbc, wri_ref, br_ref, bi_ref, lam_ref):
    ri = jnp.dot(xbc.astype(BF16), wri_ref[...], preferred_element_type=F32)
    r = _sigmoid(ri[:, :W_B] + br_ref[...])
    ig = _sigmoid(ri[:, W_B:] + bi_ref[...])
    log_a = (-LRU_C) * r * _softplus(-lam_ref[...])
    a = jnp.exp(log_a)
    mult = jnp.sqrt(1.0 - jnp.exp(2.0 * log_a))
    return a, mult, ig * xbc


def _bc_prompt_kernel(xb_ref, xc_ref, gb_ref, gc_ref, cw_ref, cb_ref, wri_ref, br_ref, bi_ref,
                      lam_ref, sw_ref, yb_ref, yc_ref, lbuf_ref, hlast_ref, sbuf_ref,
                      xs, us, a_s, b_s, h_s, hprev, *, tm, tiles_per_seq):
    c = pl.program_id(1)
    first = c == 0

    @pl.when(first)
    def _():
        xs[0:8, :] = jnp.zeros((8, W_B), F32)
        us[0:8, :] = jnp.zeros((8, W_C), F32)
        hprev[...] = jnp.zeros((8, W_B), F32)

    @pl.when(c > 0)
    def _():
        xs[0:8, :] = xs[tm:tm + 8, :]
        us[0:8, :] = us[tm:tm + 8, :]

    xs[8:8 + tm, :] = xb_ref[...]
    us[8:8 + tm, :] = gc_ref[...] * xc_ref[...]

    cw = cw_ref[...]
    xbc = cb_ref[...] + xs[8:8 + tm, :] * cw[3:4, :]
    for i in range(CONV_B - 1):
        xbc = xbc + xs[5 + i:5 + i + tm, :] * cw[i:i + 1, :]
    a, mult, ix = _lru_coeffs(xbc, wri_ref, br_ref, bi_ref, lam_ref)
    a_s[...] = a
    b_s[...] = mult * ix

    @pl.when(first)
    def _():
        b_s[0:1, :] = ix[0:1, :]

    row8 = lax.broadcasted_iota(jnp.int32, (8, W_B), 0)

    def tile_body(t, hp):
        off = pl.multiple_of(t * 8, 8)
        av = a_s[pl.ds(off, 8), :]
        bv = b_s[pl.ds(off, 8), :]
        for k in (1, 2, 4):
            a_sh = jnp.where(row8 >= k, pltpu.roll(av, k, 0), 1.0)
            b_sh = jnp.where(row8 >= k, pltpu.roll(bv, k, 0), 0.0)
            bv = av * b_sh + bv
            av = av * a_sh
        hv = av * hp + bv
        h_s[pl.ds(off, 8), :] = hv
        return jnp.broadcast_to(hv[7:8, :], (8, W_B))

    hprev[...] = lax.fori_loop(0, tm // 8, tile_body, hprev[...])
    yb_ref[...] = h_s[...].astype(yb_ref.dtype)

    sw = sw_ref[...]
    yc = us[6:6 + tm, :] * sw[0:1, :] + us[7:7 + tm, :] * sw[1:2, :] + us[8:8 + tm, :] * sw[2:3, :]
    yc_ref[...] = (gb_ref[...] * yc).astype(yc_ref.dtype)

    @pl.when(c == tiles_per_seq - 1)
    def _():
        lbuf_ref[...] = xs[8 + tm - (CONV_B - 1):8 + tm, :]
        hlast_ref[...] = h_s[tm - 1:tm, :]
        sbuf_ref[...] = us[8 + tm - (CONV_C - 1):8 + tm, :]


def _bc_prompt(proj, n_seq, tm, tiles_per_seq, cw_all, cb_all, wri_all, br_all, bi_all, lam_all,
               sw_all, l):
    m = proj.shape[0]
    col = lambda cidx: pl.BlockSpec((tm, CB), lambda b, c: (b * tiles_per_seq + c, cidx))
    par = lambda rows, width: pl.BlockSpec((None, rows, width), lambda b, c: (l, 0, 0))
    seq_out = lambda rows: pl.BlockSpec((None, rows, CB), lambda b, c: (b, 0, 0))
    row_out = pl.BlockSpec((tm, CB), lambda b, c: (b * tiles_per_seq + c, 0))
    kern = functools.partial(_bc_prompt_kernel, tm=tm, tiles_per_seq=tiles_per_seq)
    return pl.pallas_call(
        kern, grid=(n_seq, tiles_per_seq),
        in_specs=[col(COL_XB), col(COL_XC), col(COL_GB), col(COL_GC),
                  par(CONV_B, W_B), par(1, W_B), par(W_B, 2 * W_B), par(1, W_B), par(1, W_B),
                  par(1, W_B), par(CONV_C, W_C)],
        out_specs=[row_out, row_out, seq_out(CONV_B - 1), seq_out(1), seq_out(CONV_C - 1)],
        out_shape=[jax.ShapeDtypeStruct((m, W_B), BF16), jax.ShapeDtypeStruct((m, W_C), BF16),
                   jax.ShapeDtypeStruct((n_seq, CONV_B - 1, W_B), F32),
                   jax.ShapeDtypeStruct((n_seq, 1, W_B), F32),
                   jax.ShapeDtypeStruct((n_seq, CONV_C - 1, W_C), F32)],
        scratch_shapes=[pltpu.VMEM((8 + tm, W_B), F32), pltpu.VMEM((8 + tm, W_C), F32),
                        pltpu.VMEM((tm, W_B), F32), pltpu.VMEM((tm, W_B), F32),
                        pltpu.VMEM((tm, W_B), F32), pltpu.VMEM((8, W_B), F32)],
        compiler_params=_cparams(2))(proj, proj, proj, proj, cw_all, cb_all, wri_all, br_all,
                                     bi_all, lam_all, sw_all)


def _bc_sample_kernel(xb_ref, xc_ref, gb_ref, gc_ref, lbuf_in, h0_ref, sbuf_in, cw_ref, cb_ref,
                      wri_ref, br_ref, bi_ref, lam_ref, sw_ref, yb_ref, yc_ref, lbuf_ref, hlast_ref,
                      sbuf_ref, xs, us, *, nb, nt):
    m = nb * nt
    hb = (CONV_B - 1) * nb
    hc = (CONV_C - 1) * nb
    xs[0:hb, :] = lbuf_in[...]
    xs[hb:hb + m, :] = xb_ref[...]
    us[0:hc, :] = sbuf_in[...]
    us[hc:hc + m, :] = gc_ref[...] * xc_ref[...]

    cw = cw_ref[...]
    xbc = cb_ref[...] + xs[hb:hb + m, :] * cw[3:4, :]
    for i in range(CONV_B - 1):
        xbc = xbc + xs[i * nb:i * nb + m, :] * cw[i:i + 1, :]
    a, mult, ix = _lru_coeffs(xbc, wri_ref, br_ref, bi_ref, lam_ref)
    bterm = mult * ix
    h = h0_ref[...]
    for t in range(nt):
        h = a[t * nb:(t + 1) * nb, :] * h + bterm[t * nb:(t + 1) * nb, :]
        yb_ref[t * nb:(t + 1) * nb, :] = h.astype(yb_ref.dtype)
    hlast_ref[...] = h
    lbuf_ref[...] = xs[m:m + hb, :]

    sw = sw_ref[...]
    yc = us[0:m, :] * sw[0:1, :] + us[nb:nb + m, :] * sw[1:2, :] + us[2 * nb:2 * nb + m, :] * sw[2:3, :]
    yc_ref[...] = (gb_ref[...] * yc).astype(yc_ref.dtype)
    sbuf_ref[...] = us[m:m + hc, :]


def _bc_sample(proj, nb, nt, lbuf, h0, sbuf, cw_all, cb_all, wri_all, br_all, bi_all, lam_all,
               sw_all, l):
    m = nb * nt
    hb = (CONV_B - 1) * nb
    hc = (CONV_C - 1) * nb
    col = lambda cidx: pl.BlockSpec((m, CB), lambda i: (0, cidx))
    full = lambda rows: pl.BlockSpec((rows, CB), lambda i: (0, 0))
    par = lambda rows, width: pl.BlockSpec((None, rows, width), lambda i: (l, 0, 0))
    kern = functools.partial(_bc_sample_kernel, nb=nb, nt=nt)
    return pl.pallas_call(
        kern, grid=(1,),
        in_specs=[col(COL_XB), col(COL_XC), col(COL_GB), col(COL_GC), full(hb), full(nb), full(hc),
                  par(CONV_B, W_B), par(1, W_B), par(W_B, 2 * W_B), par(1, W_B), par(1, W_B),
                  par(1, W_B), par(CONV_C, W_C)],
        out_specs=[full(m), full(m), full(hb), full(nb), full(hc)],
        out_shape=[jax.ShapeDtypeStruct((m, W_B), BF16), jax.ShapeDtypeStruct((m, W_C), BF16),
                   jax.ShapeDtypeStruct((hb, W_B), F32), jax.ShapeDtypeStruct((nb, W_B), F32),
                   jax.ShapeDtypeStruct((hc, W_C), F32)],
        scratch_shapes=[pltpu.VMEM((hb + m, W_B), F32), pltpu.VMEM((hc + m, W_C), F32)],
        compiler_params=_cparams(1))(proj, proj, proj, proj, lbuf, h0, sbuf, cw_all, cb_all,
                                     wri_all, br_all, bi_all, lam_all, sw_all)


def _sb_block(qt, kblk, vblk, tri, bias, carry, acc, mask):
    z = lax.dot_general(qt, kblk, (((1,), (1,)), ((), ())), preferred_element_type=F32) + bias
    sp = _softplus(z)
    lk = -sp if mask is None else jnp.where(mask, -sp, 0.0)
    btw = (jnp.dot(lk.astype(BF16), tri, preferred_element_type=F32)
           + jnp.tile(carry, (1, lk.shape[1] // DH)))
    w = jnp.exp(z - sp + btw)
    if mask is not None:
        w = jnp.where(mask, w, 0.0)
    acc = acc + jnp.dot(w.astype(BF16), vblk, preferred_element_type=F32)
    carry = carry + jnp.broadcast_to(jnp.sum(lk, axis=1, keepdims=True), carry.shape)
    return acc, carry


def _attn_prompt_kernel(bias_ref, q_ref, k_ref, v_ref, tri_k_ref, tri_m_ref, o_ref, qb, kb, vb, *,
                        layer, tq, tk, n_real, unroll):
    bias = bias_ref[layer, pl.program_id(1)]
    qb[...] = (q_ref[...] * SCALE).astype(BF16)
    kb[...] = k_ref[...].astype(BF16)
    vb[...] = v_ref[...].astype(BF16)
    tri_k = tri_k_ref[...]
    tri_m = tri_m_ref[...]
    meta_mask = lax.broadcasted_iota(jnp.int32, (tq, PAGE), 1) < N_META
    kpq = tq // tk

    for qi in range(n_real // tq):
        q0 = N_META + tq * qi
        qt = qb[q0:q0 + tq, :]
        acc = jnp.zeros((tq, DH), F32)
        carry = jnp.zeros((tq, DH), F32)
        for d in reversed(range(kpq)):
            k0 = q0 + d * tk
            r0 = d * tk
            vis = (lax.broadcasted_iota(jnp.int32, (tq - r0, tk), 1)
                   < lax.broadcasted_iota(jnp.int32, (tq - r0, tk), 0))
            a_new, c_new = _sb_block(qt[r0:, :], kb[k0:k0 + tk, :], vb[k0:k0 + tk, :], tri_k, bias,
                                     carry[r0:, :], acc[r0:, :], vis)
            acc = a_new if r0 == 0 else jnp.concatenate([acc[:r0, :], a_new], axis=0)
            carry = c_new if r0 == 0 else jnp.concatenate([carry[:r0, :], c_new], axis=0)
        n_low = qi * kpq
        if n_low > 0:
            def body(jj, st, n_low=n_low, qt=qt):
                off = pl.multiple_of(N_META + (n_low - 1 - jj) * tk, 16)
                return _sb_block(qt, kb[pl.ds(off, tk), :], vb[pl.ds(off, tk), :], tri_k, bias,
                                 st[1], st[0], None)

            acc, carry = lax.fori_loop(0, n_low, body, (acc, carry), unroll=unroll)
        acc, carry = _sb_block(qt, kb[0:PAGE, :], vb[0:PAGE, :], tri_m, bias, carry, acc, meta_mask)
        o_ref[q0:q0 + tq, :] = acc.astype(o_ref.dtype)

    mm = (lax.broadcasted_iota(jnp.int32, (N_META, PAGE), 1)
          < lax.broadcasted_iota(jnp.int32, (N_META, PAGE), 0))
    acc, _ = _sb_block(qb[0:N_META, :], kb[0:PAGE, :], vb[0:PAGE, :], tri_m, bias,
                       jnp.zeros((N_META, DH), F32), jnp.zeros((N_META, DH), F32), mm)
    o_ref[0:N_META, :] = acc.astype(o_ref.dtype)


def _later_key_matrix(n):
    return (lax.broadcasted_iota(jnp.int32, (n, n), 0)
            > lax.broadcasted_iota(jnp.int32, (n, n), 1)).astype(BF16)


def _attn_prompt(q, k, v, sb_bias, l, n_seq, t_seq, tq, tk, unroll):
    m = q.shape[0]
    kern = functools.partial(_attn_prompt_kernel, layer=l, tq=tq, tk=tk, n_real=t_seq - N_META,
                             unroll=unroll)
    head = pl.BlockSpec((t_seq, DH), lambda b, h: (b, h))
    const = lambda n: pl.BlockSpec((n, n), lambda b, h: (0, 0))
    return pl.pallas_call(
        kern, grid=(n_seq, H_A),
        in_specs=[pl.BlockSpec(memory_space=pltpu.SMEM), head, head, head, const(tk), const(PAGE)],
        out_specs=head,
        out_shape=jax.ShapeDtypeStruct((m, W_A), BF16),
        scratch_shapes=[pltpu.VMEM((t_seq, DH), BF16)] * 3,
        compiler_params=_cparams(2))(sb_bias, q, k, v, _later_key_matrix(tk),
                                     _later_key_matrix(PAGE))


def _attn_sample_kernel(*refs, nt, pps, cpc):
    n_pg = 2 * pps
    q_ref, kn_ref, vn_ref = refs[1:4]
    k_refs = refs[4:4 + n_pg]
    v_refs = refs[4 + n_pg:4 + 2 * n_pg]
    bias_ref, tri_ref, o_ref, qf, qbd, kpad, vpad, k2s, v2s, acc, carry = refs[4 + 2 * n_pg:]
    j = pl.program_id(1)
    lanes = 2 * H_A * nt
    wk = 2 * W_A

    def step(k2, v2, tri, mask, a, c):
        z = jnp.dot(k2, qbd[...], preferred_element_type=F32) + bias_ref[...]
        sp = _softplus(z)
        lk = -sp if mask is None else jnp.where(mask, -sp, 0.0)
        bt = jnp.dot(tri, lk.astype(BF16), preferred_element_type=F32)
        w = jnp.exp(z - sp + bt + c)
        if mask is not None:
            w = jnp.where(mask, w, 0.0)
        a = a + jnp.dot(w.T.astype(BF16), v2, preferred_element_type=F32)
        return a, c + jnp.sum(lk, axis=0, keepdims=True)

    @pl.when(j == 0)
    def _():
        @pl.when(pl.program_id(0) == 0)
        def _():
            qf[...] = jnp.zeros((lanes, wk), F32)
            kpad[...] = jnp.zeros((PAGE, wk), F32)
            vpad[...] = jnp.zeros((PAGE, wk), F32)

        for b in range(2):
            kpad[0:nt, b * W_A:(b + 1) * W_A] = kn_ref[b]
            vpad[0:nt, b * W_A:(b + 1) * W_A] = vn_ref[b]
            for h in range(H_A):
                r0 = (b * H_A + h) * nt
                c0 = b * W_A + h * DH
                qf[r0:r0 + nt, c0:c0 + DH] = q_ref[b, :, h * DH:(h + 1) * DH]
        qbd[...] = (qf[...].T * SCALE).astype(BF16)
        row = lax.broadcasted_iota(jnp.int32, (PAGE, lanes), 0)
        tq = lax.broadcasted_iota(jnp.int32, (PAGE, lanes), 1) % nt
        acc[...], carry[...] = step(kpad[...].astype(BF16), vpad[...].astype(BF16),
                                    tri_ref[0:PAGE, 0:PAGE], row < tq,
                                    jnp.zeros((lanes, wk), F32), jnp.zeros((1, lanes), F32))

    for b in range(2):
        for i in range(pps):
            for h in range(H_A):
                c0 = b * W_A + h * DH
                k2s[i * PAGE:(i + 1) * PAGE, c0:c0 + DH] = (
                    k_refs[b * pps + i][pl.ds(h, PAGE, stride=H_A), :].astype(BF16))
                v2s[i * PAGE:(i + 1) * PAGE, c0:c0 + DH] = (
                    v_refs[b * pps + i][pl.ds(h, PAGE, stride=H_A), :].astype(BF16))
    rc = cpc * PAGE
    a, c = acc[...], carry[...]
    for ch in reversed(range(pps // cpc)):
        a, c = step(k2s[ch * rc:(ch + 1) * rc, :], v2s[ch * rc:(ch + 1) * rc, :], tri_ref[...], None,
                    a, c)
    acc[...], carry[...] = a, c

    @pl.when(j == pl.num_programs(1) - 1)
    def _():
        for b in range(2):
            for h in range(H_A):
                r0 = (b * H_A + h) * nt
                c0 = b * W_A + h * DH
                o_ref[b, :, h * DH:(h + 1) * DH] = acc[r0:r0 + nt, c0:c0 + DH]


def _attn_sample(q, kn, vn, cache_k, cache_v, page_table, bias_lanes, l, pps=4, cpc=2):
    nb, nt, _ = q.shape
    n_pages = page_table.shape[1]
    pt = page_table.reshape(-1)
    lanes = 2 * H_A * nt
    wk = 2 * W_A
    rows = pps * PAGE
    tok = pl.BlockSpec((2, nt, W_A), lambda p, j, pt_ref: (p, 0, 0))

    def page(which, i):
        def index_map(p, j, pt_ref):
            return (l, pt_ref[(2 * p + which) * n_pages + n_pages - (j + 1) * pps + i], 0, 0)
        return pl.BlockSpec((None, None, PAGE * H_A, DH), index_map)

    pages = [page(b, i) for b in range(2) for i in range(pps)]
    kern = functools.partial(_attn_sample_kernel, nt=nt, pps=pps, cpc=cpc)
    grid_spec = pltpu.PrefetchScalarGridSpec(
        num_scalar_prefetch=1, grid=(nb // 2, n_pages // pps),
        in_specs=[tok, tok, tok] + pages + pages + [
            pl.BlockSpec((1, lanes), lambda p, j, pt_ref: (0, 0)),
            pl.BlockSpec((cpc * PAGE, cpc * PAGE), lambda p, j, pt_ref: (0, 0))],
        out_specs=tok,
        scratch_shapes=[pltpu.VMEM((lanes, wk), F32), pltpu.VMEM((wk, lanes), BF16),
                        pltpu.VMEM((PAGE, wk), F32), pltpu.VMEM((PAGE, wk), F32),
                        pltpu.VMEM((rows, wk), BF16), pltpu.VMEM((rows, wk), BF16),
                        pltpu.VMEM((lanes, wk), F32), pltpu.VMEM((1, lanes), F32)])
    tri = _later_key_matrix(cpc * PAGE).T
    return pl.pallas_call(
        kern, grid_spec=grid_spec, out_shape=jax.ShapeDtypeStruct((nb, nt, W_A), F32),
        compiler_params=_cparams(2))(pt, q, kn, vn, *([cache_k] * (2 * pps)),
                                     *([cache_v] * (2 * pps)), bias_lanes, tri)


def _block_diag(w):
    depth, nblk, c, d = w.shape
    eye = jnp.eye(nblk, dtype=w.dtype)
    return jnp.einsum('lncd,nm->lncmd', w, eye).reshape(depth, nblk * c, nblk * d)


def kernel(x_prompt, x_sample, cache_k, cache_v, state_lru_conv, state_lru_h, state_sconv, state_ffn_conv, page_table, meta_tokens, norm_mix_pre, norm_mix_post, norm_ffn_pre, norm_ffn_post, w_in, sb_bias, lru_conv_w, lru_conv_b, lru_w_r, lru_b_r, lru_w_i, lru_b_i, lru_lambda, sconv_w, w_out_a, w_out_b, w_out_c, w_o, ffn_w_up, ffn_conv_w, ffn_conv_b, ffn_w_down):
    bp, seq, _ = x_prompt.shape
    nb, nt, _ = x_sample.shape
    t_seq = N_META + seq
    n_pool = cache_k.shape[1]

    row3 = lambda p: p.reshape(DEPTH, 1, p.shape[-1])
    g_mix_pre, g_mix_post = row3(norm_mix_pre), row3(norm_mix_post)
    g_ffn_pre, g_ffn_post = row3(norm_ffn_pre), row3(norm_ffn_post)
    lcb, lbr, lbi, lam = row3(lru_conv_b), row3(lru_b_r), row3(lru_b_i), row3(lru_lambda)
    fcb = row3(ffn_conv_b)
    wri = jnp.concatenate([_block_diag(lru_w_r), _block_diag(lru_w_i)], axis=-1).astype(BF16)
    ck = cache_k.reshape(DEPTH, n_pool, PAGE * H_A, DH)
    cv = cache_v.reshape(DEPTH, n_pool, PAGE * H_A, DH)

    tp = t_seq // 3
    ms = nb * nt

    def layer(x, xn, l, prompt):
        last = l == DEPTH - 1
        if prompt:
            tm_big, tm, tm_row = t_seq, tp, tp
        else:
            tm_big, tm, tm_row = ms, ms, ms // 4
        q = _matmul(xn, w_in, l, tm_big, CB, col0=0, n=W_A)
        k = _matmul(xn, w_in, l, tm_big, CB, col0=W_A, n=W_A)
        v = _matmul(xn, w_in, l, tm_big, CB, col0=2 * W_A, n=W_A)
        proj = _matmul(xn, w_in, l, tm_big, CB, col0=3 * W_A, n=N_IN - 3 * W_A)
        if prompt:
            ya = _attn_prompt(q, k, v, sb_bias, l, bp, t_seq, 1024, 256, 2)
            yb, yc, lbuf, hlast, sbuf = _bc_prompt(proj, bp, tm, 3, lru_conv_w, lcb, wri, lbr, lbi,
                                                   lam, sconv_w, l)
            hlast = hlast.reshape(bp, W_B)
            k_out = k.reshape(bp, t_seq, H_A, DH)
            v_out = v.reshape(bp, t_seq, H_A, DH)
        else:
            to_bm = lambda a: a.reshape(-1, nb, a.shape[-1]).transpose(1, 0, 2)
            to_tm = lambda a: a.transpose(1, 0, 2).reshape(a.shape[1] * nb, a.shape[-1])
            q_bm, k_bm, v_bm = to_bm(q), to_bm(k), to_bm(v)
            bias_lanes = jnp.tile(jnp.repeat(sb_bias[l], nt), 2).reshape(1, 2 * H_A * nt)
            ya = to_tm(_attn_sample(q_bm, k_bm, v_bm, ck, cv, page_table, bias_lanes, l)).astype(BF16)
            yb, yc, lbuf, hlast, sbuf = _bc_sample(
                proj, nb, nt, to_tm(state_lru_conv[l]), state_lru_h[l], to_tm(state_sconv[l]),
                lru_conv_w, lcb, wri, lbr, lbi, lam, sconv_w, l)
            lbuf, sbuf = to_bm(lbuf), to_bm(sbuf)
            k_out = k_bm.reshape(nb, nt, H_A, DH)
            v_out = v_bm.reshape(nb, nt, H_A, DH)
        merged = _merge(ya, yb, yc, proj, w_out_a, w_out_b, w_out_c, l, tm)
        mo = _matmul(merged, w_o, l, tm_big, CB)
        x, xn = _resid_norm(x, mo, g_mix_post, l, g_ffn_pre, l, tm_row)
        if prompt:
            act, sta, stg = _ffn_up(xn, ffn_w_up, ffn_conv_w, fcb, l, tm=tm, shift=1, halo=8,
                                    tiles_per_seq=3)
            fbuf = jnp.concatenate([sta, stg], axis=-1)
        else:
            act, sta, stg = _ffn_up(xn, ffn_w_up, ffn_conv_w, fcb, l, tm=tm, shift=nb,
                                    halo=(CONV_F - 1) * nb, tiles_per_seq=1,
                                    init=to_tm(state_ffn_conv[l]))
            fbuf = to_bm(jnp.concatenate([sta, stg], axis=-1))
        f = _matmul(act, ffn_w_down, l, tm, CB if prompt else CB // 2)
        x, xn = _resid_norm(x, f, g_ffn_post, l, None if last else g_mix_pre, l + 1, tm_row)
        return x, xn, (k_out, v_out, lbuf, hlast, sbuf, fbuf)

    def run(x, prompt, tm):
        xn = _rmsnorm_cast(x, g_mix_pre, 0, tm)
        outs = []
        for l in range(DEPTH):
            x, xn, o = layer(x, xn, l, prompt)
            outs.append(o)
        return x, [jnp.stack(z) for z in zip(*outs)]

    dt = x_prompt.dtype
    xp = jnp.concatenate([jnp.broadcast_to(meta_tokens.astype(dt)[None], (bp, N_META, D_MODEL)),
                          x_prompt], axis=1).reshape(bp * t_seq, D_MODEL)
    xp, (kp, vp, lbp, hlp, sbp, fbp) = run(xp, True, tp)
    y_prompt = xp.reshape(bp, t_seq, D_MODEL)[:, N_META:]

    xs = x_sample.transpose(1, 0, 2).reshape(ms, D_MODEL)
    xs, (ks, vs, lbs, hls, sbs, fbs) = run(xs, False, ms // 4)
    y_sample = xs.reshape(nt, nb, D_MODEL).transpose(1, 0, 2)

    return (y_prompt, y_sample, kp, vp, ks, vs, lbp, lbs, hlp, hls, sbp, sbs, fbp, fbs)
```

```python
import functools

import jax
import jax.numpy as jnp
from jax import lax
from jax.experimental import pallas as pl
from jax.experimental.pallas import tpu as pltpu

F32 = jnp.float32
BF16 = jnp.bfloat16

D_MODEL = 2048
DEPTH = 2
N_META = 16
H_A = 8
DH = 128
W_A = H_A * DH
W_B = 512
NB_LRU = 8
W_C = 512
CONV_B = 4
CONV_C = 3
CONV_F = 3
D_FF = 5632
N_BRANCH = 3
LRU_C = 8.0
EPS = 1e-6
SCALE = DH ** -0.5
N_IN = 3 * W_A + W_B + 3 * W_C + N_BRANCH * D_MODEL
PAGE = 128

CB = 512
COL_XB, COL_XC, COL_GB, COL_GC = 0, 1, 2, 3
COL_GATE_A, COL_GATE_B, COL_GATE_C = 4, 8, 12

VMEM_LIMIT_BYTES = 58 * 1024 * 1024


def _cparams(n_axes):
    return pltpu.CompilerParams(dimension_semantics=("arbitrary",) * n_axes,
                                vmem_limit_bytes=VMEM_LIMIT_BYTES)


def _sigmoid(x):
    return 1.0 / (1.0 + jnp.exp(-x))


def _softplus(x):
    return jnp.maximum(x, 0.0) + jnp.log(1.0 + jnp.exp(-jnp.abs(x)))


def _gelu_tanh(g):
    return 0.5 * g * (1.0 + jnp.tanh(0.7978845608028654 * (g + 0.044715 * (g * g * g))))


def _rms(x, g):
    return x * lax.rsqrt(jnp.mean(x * x, axis=-1, keepdims=True) + EPS) * g


def _rmsnorm_kernel(x_ref, g_ref, o_ref):
    o_ref[...] = _rms(x_ref[...], g_ref[...]).astype(o_ref.dtype)


def _rmsnorm_cast(x, g_all, l, tm):
    m = x.shape[0]
    return pl.pallas_call(
        _rmsnorm_kernel, grid=(m // tm,),
        in_specs=[pl.BlockSpec((tm, D_MODEL), lambda i: (i, 0)),
                  pl.BlockSpec((None, 1, D_MODEL), lambda i: (l, 0, 0))],
        out_specs=pl.BlockSpec((tm, D_MODEL), lambda i: (i, 0)),
        out_shape=jax.ShapeDtypeStruct((m, D_MODEL), BF16),
        compiler_params=_cparams(1))(x, g_all)


def _resid_norm_kernel(x_ref, m_ref, gpost_ref, gnext_ref, xo_ref, xn_ref):
    x1 = x_ref[...] + _rms(m_ref[...], gpost_ref[...])
    xo_ref[...] = x1
    xn_ref[...] = _rms(x1, gnext_ref[...]).astype(xn_ref.dtype)


def _resid_kernel(x_ref, m_ref, gpost_ref, xo_ref):
    xo_ref[...] = x_ref[...] + _rms(m_ref[...], gpost_ref[...])


def _resid_norm(x, m_in, gpost_all, l, gnext_all, l_next, tm):
    m = x.shape[0]
    row = pl.BlockSpec((tm, D_MODEL), lambda i: (i, 0))
    if gnext_all is None:
        return pl.pallas_call(
            _resid_kernel, grid=(m // tm,),
            in_specs=[row, row, pl.BlockSpec((None, 1, D_MODEL), lambda i: (l, 0, 0))],
            out_specs=row, out_shape=jax.ShapeDtypeStruct((m, D_MODEL), F32),
            compiler_params=_cparams(1))(x, m_in, gpost_all), None
    return pl.pallas_call(
        _resid_norm_kernel, grid=(m // tm,),
        in_specs=[row, row, pl.BlockSpec((None, 1, D_MODEL), lambda i: (l, 0, 0)),
                  pl.BlockSpec((None, 1, D_MODEL), lambda i: (l_next, 0, 0))],
        out_specs=[row, row],
        out_shape=[jax.ShapeDtypeStruct((m, D_MODEL), F32), jax.ShapeDtypeStruct((m, D_MODEL), BF16)],
        compiler_params=_cparams(1))(x, m_in, gpost_all, gnext_all)


def _mm_kernel(a_ref, w_ref, o_ref, wb_ref):
    @pl.when(pl.program_id(1) == 0)
    def _():
        wb_ref[...] = w_ref[...].astype(BF16)

    o_ref[...] = jnp.dot(a_ref[...], wb_ref[...], preferred_element_type=F32).astype(o_ref.dtype)


def _matmul(a, w_all, l, tm, tn, out_dtype=F32, col0=0, n=None):
    m, k = a.shape
    n = w_all.shape[2] if n is None else n
    j0 = col0 // tn
    return pl.pallas_call(
        _mm_kernel, grid=(n // tn, m // tm),
        in_specs=[pl.BlockSpec((tm, k), lambda j, i: (i, 0)),
                  pl.BlockSpec((None, k, tn), lambda j, i: (l, 0, j0 + j))],
        out_specs=pl.BlockSpec((tm, tn), lambda j, i: (i, j)),
        out_shape=jax.ShapeDtypeStruct((m, n), out_dtype),
        scratch_shapes=[pltpu.VMEM((k, tn), BF16)],
        compiler_params=_cparams(2))(a, w_all)


def _merge_kernel(ya_ref, yb_ref, yc_ref, ga_ref, gb_ref, gc_ref, wa_ref, wb_ref, wc_ref, o_ref,
                  wab, wbb, wcb):
    @pl.when(pl.program_id(1) == 0)
    def _():
        wab[...] = wa_ref[...].astype(BF16)
        wbb[...] = wb_ref[...].astype(BF16)
        wcb[...] = wc_ref[...].astype(BF16)

    acc = _sigmoid(ga_ref[...]) * jnp.dot(ya_ref[...], wab[...], preferred_element_type=F32)
    acc = acc + _sigmoid(gb_ref[...]) * jnp.dot(yb_ref[...], wbb[...], preferred_element_type=F32)
    acc = acc + _sigmoid(gc_ref[...]) * jnp.dot(yc_ref[...], wcb[...], preferred_element_type=F32)
    o_ref[...] = acc.astype(o_ref.dtype)


def _merge(ya, yb, yc, proj, wa_all, wb_all, wc_all, l, tm):
    m = ya.shape[0]
    tn = CB
    return pl.pallas_call(
        _merge_kernel, grid=(D_MODEL // tn, m // tm),
        in_specs=[pl.BlockSpec((tm, W_A), lambda j, i: (i, 0)),
                  pl.BlockSpec((tm, W_B), lambda j, i: (i, 0)),
                  pl.BlockSpec((tm, W_C), lambda j, i: (i, 0)),
                  pl.BlockSpec((tm, tn), lambda j, i: (i, COL_GATE_A + j)),
                  pl.BlockSpec((tm, tn), lambda j, i: (i, COL_GATE_B + j)),
                  pl.BlockSpec((tm, tn), lambda j, i: (i, COL_GATE_C + j)),
                  pl.BlockSpec((None, W_A, tn), lambda j, i: (l, 0, j)),
                  pl.BlockSpec((None, W_B, tn), lambda j, i: (l, 0, j)),
                  pl.BlockSpec((None, W_C, tn), lambda j, i: (l, 0, j))],
        out_specs=pl.BlockSpec((tm, tn), lambda j, i: (i, j)),
        out_shape=jax.ShapeDtypeStruct((m, D_MODEL), BF16),
        scratch_shapes=[pltpu.VMEM((W_A, tn), BF16), pltpu.VMEM((W_B, tn), BF16),
                        pltpu.VMEM((W_C, tn), BF16)],
        compiler_params=_cparams(2))(ya, yb, yc, proj, proj, proj, wa_all, wb_all, wc_all)


def _ffn_up_kernel(*refs, tm, shift, halo, tiles_per_seq, has_init):
    if has_init:
        (a_ref, wa_ref, wg_ref, cwa_ref, cwg_ref, cba_ref, cbg_ref, ia_ref, ig_ref,
         act_ref, sta_ref, stg_ref, wab, wgb, ua, ug) = refs
    else:
        (a_ref, wa_ref, wg_ref, cwa_ref, cwg_ref, cba_ref, cbg_ref,
         act_ref, sta_ref, stg_ref, wab, wgb, ua, ug) = refs
    i = pl.program_id(1)
    c = i % tiles_per_seq

    @pl.when(i == 0)
    def _():
        wab[...] = wa_ref[...].astype(BF16)
        wgb[...] = wg_ref[...].astype(BF16)

    if has_init:
        ua[0:halo, :] = ia_ref[...]
        ug[0:halo, :] = ig_ref[...]
    else:
        @pl.when(c == 0)
        def _():
            ua[0:halo, :] = jnp.zeros((halo, ua.shape[1]), F32)
            ug[0:halo, :] = jnp.zeros((halo, ug.shape[1]), F32)

        @pl.when(c > 0)
        def _():
            ua[0:halo, :] = ua[tm:tm + halo, :]
            ug[0:halo, :] = ug[tm:tm + halo, :]

    a = a_ref[...]
    ua[halo:halo + tm, :] = jnp.dot(a, wab[...], preferred_element_type=F32)
    ug[halo:halo + tm, :] = jnp.dot(a, wgb[...], preferred_element_type=F32)

    def conv(u, cw_ref, cb_ref):
        cw = cw_ref[...]
        y = u[halo - 2 * shift:halo - 2 * shift + tm, :] * cw[0:1, :]
        y = y + u[halo - shift:halo - shift + tm, :] * cw[1:2, :]
        y = y + u[halo:halo + tm, :] * cw[2:3, :]
        return y + cb_ref[...]

    act = _gelu_tanh(conv(ug, cwg_ref, cbg_ref)) * conv(ua, cwa_ref, cba_ref)
    act_ref[...] = act.astype(act_ref.dtype)

    ns = (CONV_F - 1) * shift

    @pl.when(c == tiles_per_seq - 1)
    def _():
        sta_ref[...] = ua[halo + tm - ns:halo + tm, :]
        stg_ref[...] = ug[halo + tm - ns:halo + tm, :]


def _ffn_up(a, w_up_all, cw_all, cb_all, l, *, tm, shift, halo, tiles_per_seq, init=None):
    m = a.shape[0]
    tn = CB
    nj = D_FF // tn
    n_seq_tiles = m // tm
    has_init = init is not None
    ns = (CONV_F - 1) * shift
    in_specs = [pl.BlockSpec((tm, D_MODEL), lambda j, i: (i, 0)),
                pl.BlockSpec((None, D_MODEL, tn), lambda j, i: (l, 0, j)),
                pl.BlockSpec((None, D_MODEL, tn), lambda j, i: (l, 0, nj + j)),
                pl.BlockSpec((None, CONV_F, tn), lambda j, i: (l, 0, j)),
                pl.BlockSpec((None, CONV_F, tn), lambda j, i: (l, 0, nj + j)),
                pl.BlockSpec((None, 1, tn), lambda j, i: (l, 0, j)),
                pl.BlockSpec((None, 1, tn), lambda j, i: (l, 0, nj + j))]
    args = [a, w_up_all, w_up_all, cw_all, cw_all, cb_all, cb_all]
    if has_init:
        in_specs += [pl.BlockSpec((halo, tn), lambda j, i: (0, j)),
                     pl.BlockSpec((halo, tn), lambda j, i: (0, nj + j))]
        args += [init, init]
        st_shape = jax.ShapeDtypeStruct((ns, D_FF), F32)
        st_spec = pl.BlockSpec((ns, tn), lambda j, i: (0, j))
    else:
        n_seq = n_seq_tiles // tiles_per_seq
        st_shape = jax.ShapeDtypeStruct((n_seq, ns, D_FF), F32)
        st_spec = pl.BlockSpec((None, ns, tn), lambda j, i: (i // tiles_per_seq, 0, j))
    kern = functools.partial(_ffn_up_kernel, tm=tm, shift=shift, halo=halo,
                             tiles_per_seq=tiles_per_seq, has_init=has_init)
    return pl.pallas_call(
        kern, grid=(nj, n_seq_tiles),
        in_specs=in_specs,
        out_specs=[pl.BlockSpec((tm, tn), lambda j, i: (i, j)), st_spec, st_spec],
        out_shape=[jax.ShapeDtypeStruct((m, D_FF), BF16), st_shape, st_shape],
        scratch_shapes=[pltpu.VMEM((D_MODEL, tn), BF16), pltpu.VMEM((D_MODEL, tn), BF16),
                        pltpu.VMEM((halo + tm, tn), F32), pltpu.VMEM((halo + tm, tn), F32)],
        compiler_params=_cparams(2))(*args)


def _lru_coeffs(xbc, wri_ref, br_ref, bi_ref, lam_ref):
    ri = jnp.dot(xbc.astype(BF16), wri_ref[...], preferred_element_type=F32)
    r = _sigmoid(ri[:, :W_B] + br_ref[...])
    ig = _sigmoid(ri[:, W_B:] + bi_ref[...])
    log_a = (-LRU_C) * r * _softplus(-lam_ref[...])
    a = jnp.exp(log_a)
    mult = jnp.sqrt(1.0 - jnp.exp(2.0 * log_a))
    return a, mult, ig * xbc


def _bc_prompt_kernel(xb_ref, xc_ref, gb_ref, gc_ref, cw_ref, cb_ref, wri_ref, br_ref, bi_ref,
                      lam_ref, sw_ref, yb_ref, yc_ref, lbuf_ref, hlast_ref, sbuf_ref,
                      xs, us, a_s, b_s, h_s, hprev, *, tm, tiles_per_seq):
    c = pl.program_id(1)
    first = c == 0

    @pl.when(first)
    def _():
        xs[0:8, :] = jnp.zeros((8, W_B), F32)
        us[0:8, :] = jnp.zeros((8, W_C), F32)
        hprev[...] = jnp.zeros((8, W_B), F32)

    @pl.when(c > 0)
    def _():
        xs[0:8, :] = xs[tm:tm + 8, :]
        us[0:8, :] = us[tm:tm + 8, :]

    xs[8:8 + tm, :] = xb_ref[...]
    us[8:8 + tm, :] = gc_ref[...] * xc_ref[...]

    cw = cw_ref[...]
    xbc = cb_ref[...] + xs[8:8 + tm, :] * cw[3:4, :]
    for i in range(CONV_B - 1):
        xbc = xbc + xs[5 + i:5 + i + tm, :] * cw[i:i + 1, :]
    a, mult, ix = _lru_coeffs(xbc, wri_ref, br_ref, bi_ref, lam_ref)
    a_s[...] = a
    b_s[...] = mult * ix

    @pl.when(first)
    def _():
        b_s[0:1, :] = ix[0:1, :]

    row8 = lax.broadcasted_iota(jnp.int32, (8, W_B), 0)

    def tile_body(t, hp):
        off = pl.multiple_of(t * 8, 8)
        av = a_s[pl.ds(off, 8), :]
        bv = b_s[pl.ds(off, 8), :]
        for k in (1, 2, 4):
            a_sh = jnp.where(row8 >= k, pltpu.roll(av, k, 0), 1.0)
            b_sh = jnp.where(row8 >= k, pltpu.roll(bv, k, 0), 0.0)
            bv = av * b_sh + bv
            av = av * a_sh
        hv = av * hp + bv
        h_s[pl.ds(off, 8), :] = hv
        return jnp.broadcast_to(hv[7:8, :], (8, W_B))

    hprev[...] = lax.fori_loop(0, tm // 8, tile_body, hprev[...])
    yb_ref[...] = h_s[...].astype(yb_ref.dtype)

    sw = sw_ref[...]
    yc = us[6:6 + tm, :] * sw[0:1, :] + us[7:7 + tm, :] * sw[1:2, :] + us[8:8 + tm, :] * sw[2:3, :]
    yc_ref[...] = (gb_ref[...] * yc).astype(yc_ref.dtype)

    @pl.when(c == tiles_per_seq - 1)
    def _():
        lbuf_ref[...] = xs[8 + tm - (CONV_B - 1):8 + tm, :]
        hlast_ref[...] = h_s[tm - 1:tm, :]
        sbuf_ref[...] = us[8 + tm - (CONV_C - 1):8 + tm, :]


def _bc_prompt(proj, n_seq, tm, tiles_per_seq, cw_all, cb_all, wri_all, br_all, bi_all, lam_all,
               sw_all, l):
    m = proj.shape[0]
    col = lambda cidx: pl.BlockSpec((tm, CB), lambda b, c: (b * tiles_per_seq + c, cidx))
    par = lambda rows, width: pl.BlockSpec((None, rows, width), lambda b, c: (l, 0, 0))
    seq_out = lambda rows: pl.BlockSpec((None, rows, CB), lambda b, c: (b, 0, 0))
    row_out = pl.BlockSpec((tm, CB), lambda b, c: (b * tiles_per_seq + c, 0))
    kern = functools.partial(_bc_prompt_kernel, tm=tm, tiles_per_seq=tiles_per_seq)
    return pl.pallas_call(
        kern, grid=(n_seq, tiles_per_seq),
        in_specs=[col(COL_XB), col(COL_XC), col(COL_GB), col(COL_GC),
                  par(CONV_B, W_B), par(1, W_B), par(W_B, 2 * W_B), par(1, W_B), par(1, W_B),
                  par(1, W_B), par(CONV_C, W_C)],
        out_specs=[row_out, row_out, seq_out(CONV_B - 1), seq_out(1), seq_out(CONV_C - 1)],
        out_shape=[jax.ShapeDtypeStruct((m, W_B), BF16), jax.ShapeDtypeStruct((m, W_C), BF16),
                   jax.ShapeDtypeStruct((n_seq, CONV_B - 1, W_B), F32),
                   jax.ShapeDtypeStruct((n_seq, 1, W_B), F32),
                   jax.ShapeDtypeStruct((n_seq, CONV_C - 1, W_C), F32)],
        scratch_shapes=[pltpu.VMEM((8 + tm, W_B), F32), pltpu.VMEM((8 + tm, W_C), F32),
                        pltpu.VMEM((tm, W_B), F32), pltpu.VMEM((tm, W_B), F32),
                        pltpu.VMEM((tm, W_B), F32), pltpu.VMEM((8, W_B), F32)],
        compiler_params=_cparams(2))(proj, proj, proj, proj, cw_all, cb_all, wri_all, br_all,
                                     bi_all, lam_all, sw_all)


def _bc_sample_kernel(xb_ref, xc_ref, gb_ref, gc_ref, lbuf_in, h0_ref, sbuf_in, cw_ref, cb_ref,
                      wri_ref, br_ref, bi_ref, lam_ref, sw_ref, yb_ref, yc_ref, lbuf_ref, hlast_ref,
                      sbuf_ref, xs, us, *, nb, nt):
    m = nb * nt
    hb = (CONV_B - 1) * nb
    hc = (CONV_C - 1) * nb
    xs[0:hb, :] = lbuf_in[...]
    xs[hb:hb + m, :] = xb_ref[...]
    us[0:hc, :] = sbuf_in[...]
    us[hc:hc + m, :] = gc_ref[...] * xc_ref[...]

    cw = cw_ref[...]
    xbc = cb_ref[...] + xs[hb:hb + m, :] * cw[3:4, :]
    for i in range(CONV_B - 1):
        xbc = xbc + xs[i * nb:i * nb + m, :] * cw[i:i + 1, :]
    a, mult, ix = _lru_coeffs(xbc, wri_ref, br_ref, bi_ref, lam_ref)
    bterm = mult * ix
    h = h0_ref[...]
    for t in range(nt):
        h = a[t * nb:(t + 1) * nb, :] * h + bterm[t * nb:(t + 1) * nb, :]
        yb_ref[t * nb:(t + 1) * nb, :] = h.astype(yb_ref.dtype)
    hlast_ref[...] = h
    lbuf_ref[...] = xs[m:m + hb, :]

    sw = sw_ref[...]
    yc = us[0:m, :] * sw[0:1, :] + us[nb:nb + m, :] * sw[1:2, :] + us[2 * nb:2 * nb + m, :] * sw[2:3, :]
    yc_ref[...] = (gb_ref[...] * yc).astype(yc_ref.dtype)
    sbuf_ref[...] = us[m:m + hc, :]


def _bc_sample(proj, nb, nt, lbuf, h0, sbuf, cw_all, cb_all, wri_all, br_all, bi_all, lam_all,
               sw_all, l):
    m = nb * nt
    hb = (CONV_B - 1) * nb
    hc = (CONV_C - 1) * nb
    col = lambda cidx: pl.BlockSpec((m, CB), lambda i: (0, cidx))
    full = lambda rows: pl.BlockSpec((rows, CB), lambda i: (0, 0))
    par = lambda rows, width: pl.BlockSpec((None, rows, width), lambda i: (l, 0, 0))
    kern = functools.partial(_bc_sample_kernel, nb=nb, nt=nt)
    return pl.pallas_call(
        kern, grid=(1,),
        in_specs=[col(COL_XB), col(COL_XC), col(COL_GB), col(COL_GC), full(hb), full(nb), full(hc),
                  par(CONV_B, W_B), par(1, W_B), par(W_B, 2 * W_B), par(1, W_B), par(1, W_B),
                  par(1, W_B), par(CONV_C, W_C)],
        out_specs=[full(m), full(m), full(hb), full(nb), full(hc)],
        out_shape=[jax.ShapeDtypeStruct((m, W_B), BF16), jax.ShapeDtypeStruct((m, W_C), BF16),
                   jax.ShapeDtypeStruct((hb, W_B), F32), jax.ShapeDtypeStruct((nb, W_B), F32),
                   jax.ShapeDtypeStruct((hc, W_C), F32)],
        scratch_shapes=[pltpu.VMEM((hb + m, W_B), F32), pltpu.VMEM((hc + m, W_C), F32)],
        compiler_params=_cparams(1))(proj, proj, proj, proj, lbuf, h0, sbuf, cw_all, cb_all,
                                     wri_all, br_all, bi_all, lam_all, sw_all)


def _sb_block(qt, kblk, vblk, tri, bias, carry, acc, mask):
    z = lax.dot_general(qt, kblk, (((1,), (1,)), ((), ())), preferred_element_type=F32) + bias
    sp = _softplus(z)
    lk = -sp if mask is None else jnp.where(mask, -sp, 0.0)
    btw = (jnp.dot(lk.astype(BF16), tri, preferred_element_type=F32)
           + jnp.tile(carry, (1, lk.shape[1] // DH)))
    w = jnp.exp(z - sp + btw)
    if mask is not None:
        w = jnp.where(mask, w, 0.0)
    acc = acc + jnp.dot(w.astype(BF16), vblk, preferred_element_type=F32)
    carry = carry + jnp.broadcast_to(jnp.sum(lk, axis=1, keepdims=True), carry.shape)
    return acc, carry


def _attn_prompt_kernel(bias_ref, q_ref, k_ref, v_ref, tri_k_ref, tri_m_ref, o_ref, qb, kb, vb, *,
                        layer, tq, tk, n_real, unroll):
    bias = bias_ref[layer, pl.program_id(1)]
    qb[...] = (q_ref[...] * SCALE).astype(BF16)
    kb[...] = k_ref[...].astype(BF16)
    vb[...] = v_ref[...].astype(BF16)
    tri_k = tri_k_ref[...]
    tri_m = tri_m_ref[...]
    meta_mask = lax.broadcasted_iota(jnp.int32, (tq, PAGE), 1) < N_META
    kpq = tq // tk

    for qi in range(n_real // tq):
        q0 = N_META + tq * qi
        qt = qb[q0:q0 + tq, :]
        acc = jnp.zeros((tq, DH), F32)
        carry = jnp.zeros((tq, DH), F32)
        for d in reversed(range(kpq)):
            k0 = q0 + d * tk
            r0 = d * tk
            vis = (lax.broadcasted_iota(jnp.int32, (tq - r0, tk), 1)
                   < lax.broadcasted_iota(jnp.int32, (tq - r0, tk), 0))
            a_new, c_new = _sb_block(qt[r0:, :], kb[k0:k0 + tk, :], vb[k0:k0 + tk, :], tri_k, bias,
                                     carry[r0:, :], acc[r0:, :], vis)
            acc = a_new if r0 == 0 else jnp.concatenate([acc[:r0, :], a_new], axis=0)
            carry = c_new if r0 == 0 else jnp.concatenate([carry[:r0, :], c_new], axis=0)
        n_low = qi * kpq
        if n_low > 0:
            def body(jj, st, n_low=n_low, qt=qt):
                off = pl.multiple_of(N_META + (n_low - 1 - jj) * tk, 16)
                return _sb_block(qt, kb[pl.ds(off, tk), :], vb[pl.ds(off, tk), :], tri_k, bias,
                                 st[1], st[0], None)

            acc, carry = lax.fori_loop(0, n_low, body, (acc, carry), unroll=unroll)
        acc, carry = _sb_block(qt, kb[0:PAGE, :], vb[0:PAGE, :], tri_m, bias, carry, acc, meta_mask)
        o_ref[q0:q0 + tq, :] = acc.astype(o_ref.dtype)

    mm = (lax.broadcasted_iota(jnp.int32, (N_META, PAGE), 1)
          < lax.broadcasted_iota(jnp.int32, (N_META, PAGE), 0))
    acc, _ = _sb_block(qb[0:N_META, :], kb[0:PAGE, :], vb[0:PAGE, :], tri_m, bias,
                       jnp.zeros((N_META, DH), F32), jnp.zeros((N_META, DH), F32), mm)
    o_ref[0:N_META, :] = acc.astype(o_ref.dtype)


def _later_key_matrix(n):
    return (lax.broadcasted_iota(jnp.int32, (n, n), 0)
            > lax.broadcasted_iota(jnp.int32, (n, n), 1)).astype(BF16)


def _attn_prompt(q, k, v, sb_bias, l, n_seq, t_seq, tq, tk, unroll):
    m = q.shape[0]
    kern = functools.partial(_attn_prompt_kernel, layer=l, tq=tq, tk=tk, n_real=t_seq - N_META,
                             unroll=unroll)
    head = pl.BlockSpec((t_seq, DH), lambda b, h: (b, h))
    const = lambda n: pl.BlockSpec((n, n), lambda b, h: (0, 0))
    return pl.pallas_call(
        kern, grid=(n_seq, H_A),
        in_specs=[pl.BlockSpec(memory_space=pltpu.SMEM), head, head, head, const(tk), const(PAGE)],
        out_specs=head,
        out_shape=jax.ShapeDtypeStruct((m, W_A), BF16),
        scratch_shapes=[pltpu.VMEM((t_seq, DH), BF16)] * 3,
        compiler_params=_cparams(2))(sb_bias, q, k, v, _later_key_matrix(tk),
                                     _later_key_matrix(PAGE))


def _attn_sample_kernel(*refs, nt, pps, cpc):
    n_pg = 2 * pps
    q_ref, kn_ref, vn_ref = refs[1:4]
    k_refs = refs[4:4 + n_pg]
    v_refs = refs[4 + n_pg:4 + 2 * n_pg]
    bias_ref, tri_ref, o_ref, qf, qbd, kpad, vpad, k2s, v2s, acc, carry = refs[4 + 2 * n_pg:]
    j = pl.program_id(1)
    lanes = 2 * H_A * nt
    wk = 2 * W_A

    def step(k2, v2, tri, mask, a, c):
        z = jnp.dot(k2, qbd[...], preferred_element_type=F32) + bias_ref[...]
        sp = _softplus(z)
        lk = -sp if mask is None else jnp.where(mask, -sp, 0.0)
        bt = jnp.dot(tri, lk.astype(BF16), preferred_element_type=F32)
        w = jnp.exp(z - sp + bt + c)
        if mask is not None:
            w = jnp.where(mask, w, 0.0)
        a = a + jnp.dot(w.T.astype(BF16), v2, preferred_element_type=F32)
        return a, c + jnp.sum(lk, axis=0, keepdims=True)

    @pl.when(j == 0)
    def _():
        @pl.when(pl.program_id(0) == 0)
        def _():
            qf[...] = jnp.zeros((lanes, wk), F32)
            kpad[...] = jnp.zeros((PAGE, wk), F32)
            vpad[...] = jnp.zeros((PAGE, wk), F32)

        for b in range(2):
            kpad[0:nt, b * W_A:(b + 1) * W_A] = kn_ref[b]
            vpad[0:nt, b * W_A:(b + 1) * W_A] = vn_ref[b]
            for h in range(H_A):
                r0 = (b * H_A + h) * nt
                c0 = b * W_A + h * DH
                qf[r0:r0 + nt, c0:c0 + DH] = q_ref[b, :, h * DH:(h + 1) * DH]
        qbd[...] = (qf[...].T * SCALE).astype(BF16)
        row = lax.broadcasted_iota(jnp.int32, (PAGE, lanes), 0)
        tq = lax.broadcasted_iota(jnp.int32, (PAGE, lanes), 1) % nt
        acc[...], carry[...] = step(kpad[...].astype(BF16), vpad[...].astype(BF16),
                                    tri_ref[0:PAGE, 0:PAGE], row < tq,
                                    jnp.zeros((lanes, wk), F32), jnp.zeros((1, lanes), F32))

    for b in range(2):
        for i in range(pps):
            for h in range(H_A):
                c0 = b * W_A + h * DH
                k2s[i * PAGE:(i + 1) * PAGE, c0:c0 + DH] = (
                    k_refs[b * pps + i][pl.ds(h, PAGE, stride=H_A), :].astype(BF16))
                v2s[i * PAGE:(i + 1) * PAGE, c0:c0 + DH] = (
                    v_refs[b * pps + i][pl.ds(h, PAGE, stride=H_A), :].astype(BF16))
    rc = cpc * PAGE
    a, c = acc[...], carry[...]
    for ch in reversed(range(pps // cpc)):
        a, c = step(k2s[ch * rc:(ch + 1) * rc, :], v2s[ch * rc:(ch + 1) * rc, :], tri_ref[...], None,
                    a, c)
    acc[...], carry[...] = a, c

    @pl.when(j == pl.num_programs(1) - 1)
    def _():
        for b in range(2):
            for h in range(H_A):
                r0 = (b * H_A + h) * nt
                c0 = b * W_A + h * DH
                o_ref[b, :, h * DH:(h + 1) * DH] = acc[r0:r0 + nt, c0:c0 + DH]


def _attn_sample(q, kn, vn, cache_k, cache_v, page_table, bias_lanes, l, pps=4, cpc=2):
    nb, nt, _ = q.shape
    n_pages = page_table.shape[1]
    pt = page_table.reshape(-1)
    lanes = 2 * H_A * nt
    wk = 2 * W_A
    rows = pps * PAGE
    tok = pl.BlockSpec((2, nt, W_A), lambda p, j, pt_ref: (p, 0, 0))

    def page(which, i):
        def index_map(p, j, pt_ref):
            return (l, pt_ref[(2 * p + which) * n_pages + n_pages - (j + 1) * pps + i], 0, 0)
        return pl.BlockSpec((None, None, PAGE * H_A, DH), index_map)

    pages = [page(b, i) for b in range(2) for i in range(pps)]
    kern = functools.partial(_attn_sample_kernel, nt=nt, pps=pps, cpc=cpc)
    grid_spec = pltpu.PrefetchScalarGridSpec(
        num_scalar_prefetch=1, grid=(nb // 2, n_pages // pps),
        in_specs=[tok, tok, tok] + pages + pages + [
            pl.BlockSpec((1, lanes), lambda p, j, pt_ref: (0, 0)),
            pl.BlockSpec((cpc * PAGE, cpc * PAGE), lambda p, j, pt_ref: (0, 0))],
        out_specs=tok,
        scratch_shapes=[pltpu.VMEM((lanes, wk), F32), pltpu.VMEM((wk, lanes), BF16),
                        pltpu.VMEM((PAGE, wk), F32), pltpu.VMEM((PAGE, wk), F32),
                        pltpu.VMEM((rows, wk), BF16), pltpu.VMEM((rows, wk), BF16),
                        pltpu.VMEM((lanes, wk), F32), pltpu.VMEM((1, lanes), F32)])
    tri = _later_key_matrix(cpc * PAGE).T
    return pl.pallas_call(
        kern, grid_spec=grid_spec, out_shape=jax.ShapeDtypeStruct((nb, nt, W_A), F32),
        compiler_params=_cparams(2))(pt, q, kn, vn, *([cache_k] * (2 * pps)),
                                     *([cache_v] * (2 * pps)), bias_lanes, tri)


def _block_diag(w):
    depth, nblk, c, d = w.shape
    eye = jnp.eye(nblk, dtype=w.dtype)
    return jnp.einsum('lncd,nm->lncmd', w, eye).reshape(depth, nblk * c, nblk * d)


def kernel(x_prompt, x_sample, cache_k, cache_v, state_lru_conv, state_lru_h, state_sconv, state_ffn_conv, page_table, meta_tokens, norm_mix_pre, norm_mix_post, norm_ffn_pre, norm_ffn_post, w_in, sb_bias, lru_conv_w, lru_conv_b, lru_w_r, lru_b_r, lru_w_i, lru_b_i, lru_lambda, sconv_w, w_out_a, w_out_b, w_out_c, w_o, ffn_w_up, ffn_conv_w, ffn_conv_b, ffn_w_down):
    bp, seq, _ = x_prompt.shape
    nb, nt, _ = x_sample.shape
    t_seq = N_META + seq
    n_pool = cache_k.shape[1]

    row3 = lambda p: p.reshape(DEPTH, 1, p.shape[-1])
    g_mix_pre, g_mix_post = row3(norm_mix_pre), row3(norm_mix_post)
    g_ffn_pre, g_ffn_post = row3(norm_ffn_pre), row3(norm_ffn_post)
    lcb, lbr, lbi, lam = row3(lru_conv_b), row3(lru_b_r), row3(lru_b_i), row3(lru_lambda)
    fcb = row3(ffn_conv_b)
    wri = jnp.concatenate([_block_diag(lru_w_r), _block_diag(lru_w_i)], axis=-1).astype(BF16)
    ck = cache_k.reshape(DEPTH, n_pool, PAGE * H_A, DH)
    cv = cache_v.reshape(DEPTH, n_pool, PAGE * H_A, DH)

    tp = t_seq // 3
    ms = nb * nt

    def layer(x, xn, l, prompt):
        last = l == DEPTH - 1
        if prompt:
            tm_big, tm, tm_row = t_seq, tp, tp
        else:
            tm_big, tm, tm_row = ms, ms, ms // 4
        q = _matmul(xn, w_in, l, tm_big, CB, col0=0, n=W_A)
        k = _matmul(xn, w_in, l, tm_big, CB, col0=W_A, n=W_A)
        v = _matmul(xn, w_in, l, tm_big, CB, col0=2 * W_A, n=W_A)
        proj = _matmul(xn, w_in, l, tm_big, CB, col0=3 * W_A, n=N_IN - 3 * W_A)
        if prompt:
            ya = _attn_prompt(q, k, v, sb_bias, l, bp, t_seq, 1024, 256, 2)
            yb, yc, lbuf, hlast, sbuf = _bc_prompt(proj, bp, tm, 3, lru_conv_w, lcb, wri, lbr, lbi,
                                                   lam, sconv_w, l)
            hlast = hlast.reshape(bp, W_B)
            k_out = k.reshape(bp, t_seq, H_A, DH)
            v_out = v.reshape(bp, t_seq, H_A, DH)
        else:
            to_bm = lambda a: a.reshape(-1, nb, a.shape[-1]).transpose(1, 0, 2)
            to_tm = lambda a: a.transpose(1, 0, 2).reshape(a.shape[1] * nb, a.shape[-1])
            q_bm, k_bm, v_bm = to_bm(q), to_bm(k), to_bm(v)
            bias_lanes = jnp.tile(jnp.repeat(sb_bias[l], nt), 2).reshape(1, 2 * H_A * nt)
            ya = to_tm(_attn_sample(q_bm, k_bm, v_bm, ck, cv, page_table, bias_lanes, l,
                                    pps=8, cpc=4)).astype(BF16)
            yb, yc, lbuf, hlast, sbuf = _bc_sample(
                proj, nb, nt, to_tm(state_lru_conv[l]), state_lru_h[l], to_tm(state_sconv[l]),
                lru_conv_w, lcb, wri, lbr, lbi, lam, sconv_w, l)
            lbuf, sbuf = to_bm(lbuf), to_bm(sbuf)
            k_out = k_bm.reshape(nb, nt, H_A, DH)
            v_out = v_bm.reshape(nb, nt, H_A, DH)
        merged = _merge(ya, yb, yc, proj, w_out_a, w_out_b, w_out_c, l, tm)
        mo = _matmul(merged, w_o, l, tm_big, CB)
        x, xn = _resid_norm(x, mo, g_mix_post, l, g_ffn_pre, l, tm_row)
        if prompt:
            act, sta, stg = _ffn_up(xn, ffn_w_up, ffn_conv_w, fcb, l, tm=tm, shift=1, halo=8,
                                    tiles_per_seq=3)
            fbuf = jnp.concatenate([sta, stg], axis=-1)
        else:
            act, sta, stg = _ffn_up(xn, ffn_w_up, ffn_conv_w, fcb, l, tm=tm, shift=nb,
                                    halo=(CONV_F - 1) * nb, tiles_per_seq=1,
                                    init=to_tm(state_ffn_conv[l]))
            fbuf = to_bm(jnp.concatenate([sta, stg], axis=-1))
        f = _matmul(act, ffn_w_down, l, tm, CB if prompt else CB // 2)
        x, xn = _resid_norm(x, f, g_ffn_post, l, None if last else g_mix_pre, l + 1, tm_row)
        return x, xn, (k_out, v_out, lbuf, hlast, sbuf, fbuf)

    def run(x, prompt, tm):
        xn = _rmsnorm_cast(x, g_mix_pre, 0, tm)
        outs = []
        for l in range(DEPTH):
            x, xn, o = layer(x, xn, l, prompt)
            outs.append(o)
        return x, [jnp.stack(z) for z in zip(*outs)]

    dt = x_prompt.dtype
    xp = jnp.concatenate([jnp.broadcast_to(meta_tokens.astype(dt)[None], (bp, N_META, D_MODEL)),
                          x_prompt], axis=1).reshape(bp * t_seq, D_MODEL)
    xp, (kp, vp, lbp, hlp, sbp, fbp) = run(xp, True, tp)
    y_prompt = xp.reshape(bp, t_seq, D_MODEL)[:, N_META:]

    xs = x_sample.transpose(1, 0, 2).reshape(ms, D_MODEL)
    xs, (ks, vs, lbs, hls, sbs, fbs) = run(xs, False, ms // 4)
    y_sample = xs.reshape(nt, nb, D_MODEL).transpose(1, 0, 2)

    return (y_prompt, y_sample, kp, vp, ks, vs, lbp, lbs, hlp, hls, sbp, sbs, fbp, fbs)
```
